```python
import jax, jax.numpy as jnp
from jax import lax
import numpy as np

D_MODEL = 1024
BATCH = 4
SEQ = 8192
DEPTH = 2

MIX_WIDTH = D_MODEL
SGU_WIDTH = MIX_WIDTH // 2
MLSTM_WIDTH = MIX_WIDTH - SGU_WIDTH
SGU_HEADS = 4
SGU_HEAD_DIM = SGU_WIDTH // SGU_HEADS
SGU_CHUNK = 128
MLSTM_HEADS = 4
MLSTM_HEAD_DIM = MLSTM_WIDTH // MLSTM_HEADS
MLSTM_CHUNK = 128
QKV_BLOCK = 4
CONV_WIDTH = 5
FFN_HIDDEN = ((8 * D_MODEL // 3 + 255) // 256) * 256
IN_WIDTH = 2 * SGU_WIDTH + 2 * MLSTM_WIDTH
EPS = 1e-6

kernel_name = 'hybrid_gmlp_mlstm_macaron_encoder'


def rms_norm(x, g):
    xf = x.astype(jnp.float32)
    y = xf * lax.rsqrt(jnp.mean(xf * xf, axis=-1, keepdims=True) + EPS)
    return (y * g.astype(jnp.float32)).astype(x.dtype)


def head_layer_norm(x, g):
    xf = x.astype(jnp.float32)
    mu = jnp.mean(xf, axis=-1, keepdims=True)
    var = jnp.mean(jnp.square(xf - mu), axis=-1, keepdims=True)
    y = (xf - mu) * lax.rsqrt(var + EPS) * g.astype(jnp.float32)
    return y.astype(x.dtype)


def swiglu(x, w_gate, w_up, w_down):
    return (jax.nn.silu(x @ w_gate) * (x @ w_up)) @ w_down


def spatial_gating(u, v, g, w_s, b_s):
    B, S, _ = u.shape
    nc = S // SGU_CHUNK
    u = jax.nn.gelu(u)
    v = jax.nn.gelu(v)
    vh = head_layer_norm(v.reshape(B, nc, SGU_CHUNK, SGU_HEADS, SGU_HEAD_DIM), g)
    s = jnp.einsum('hpq,bcqhd->bcphd', w_s, vh) + b_s.T[:, :, None]
    return u * s.reshape(B, S, SGU_WIDTH)


def depthwise_conv(x, w, b):
    C = x.shape[-1]
    pad = CONV_WIDTH // 2
    y = lax.conv_general_dilated(x, w[:, None, :], window_strides=(1,), padding=[(pad, pad)],
                                 dimension_numbers=('NWC', 'WIO', 'NWC'), feature_group_count=C)
    return y + b


def headwise(x, w):
    B, S, C = x.shape
    xb = x.reshape(B, S, C // QKV_BLOCK, QKV_BLOCK)
    return jnp.einsum('bsgi,gio->bsgo', xb, w).reshape(B, S, C)


def mlstm_chunkwise(q, k, v, ig, lf):
    B, S, H, Dh = q.shape
    L = MLSTM_CHUNK
    nc = S // L
    to_chunks = lambda t: t.reshape(B, nc, L, H, Dh).transpose(1, 0, 3, 2, 4)
    g_chunks = lambda t: t.reshape(B, nc, L, H).transpose(1, 0, 3, 2)
    tri = jnp.tril(jnp.ones((L, L), dtype=bool))

    def step(carry, inp):
        C, n, m = carry
        qc, kc, vc, ic, fc = inp
        b = jnp.cumsum(fc, axis=-1)
        d = jnp.where(tri, b[..., :, None] - b[..., None, :] + ic[..., None, :], -jnp.inf)
        inter = b + m[..., None]
        m_j = jnp.maximum(inter, jnp.max(d, axis=-1))
        w_intra = jnp.exp(d - m_j[..., None])
        w_inter = jnp.exp(inter - m_j)
        s = jnp.einsum('bhjd,bhsd->bhjs', qc, kc) * w_intra
        num = (w_inter[..., None] * jnp.einsum('bhvk,bhjk->bhjv', C, qc)
               + jnp.einsum('bhjs,bhsv->bhjv', s, vc))
        nq = w_inter * jnp.einsum('bhk,bhjk->bhj', n, qc) + jnp.sum(s, axis=-1)
        h = num / jnp.maximum(jnp.abs(nq), jnp.exp(-m_j))[..., None]
        b_last = b[..., -1]
        dec = b_last[..., None] - b + ic
        m_new = jnp.maximum(b_last + m, jnp.max(dec, axis=-1))
        wk = jnp.exp(dec - m_new[..., None])
        scale = jnp.exp(b_last + m - m_new)
        C_new = scale[..., None, None] * C + jnp.einsum('bhs,bhsv,bhsk->bhvk', wk, vc, kc)
        n_new = scale[..., None] * n + jnp.einsum('bhs,bhsk->bhk', wk, kc)
        return (C_new, n_new, m_new), h

    init = (jnp.zeros((B, H, Dh, Dh), jnp.float32),
            jnp.zeros((B, H, Dh), jnp.float32),
            jnp.zeros((B, H), jnp.float32))
    _, hs = lax.scan(step, init, (to_chunks(q), to_chunks(k), to_chunks(v), g_chunks(ig), g_chunks(lf)))
    return hs.transpose(1, 0, 3, 2, 4).reshape(B, S, H, Dh)


def mlstm_mixer(xm, og, conv_w, conv_b, w_q, w_k, w_v, gate_w_fwd, gate_b_fwd,
                gate_w_bwd, gate_b_bwd, mh_norm, skip):
    B, S, _ = xm.shape
    H, Dh = MLSTM_HEADS, MLSTM_HEAD_DIM
    xc = jax.nn.silu(depthwise_conv(xm, conv_w, conv_b))
    q = headwise(xc, w_q)
    k = headwise(xc, w_k) * (Dh ** -0.5)
    v = headwise(xm, w_v)
    qkv = jnp.concatenate([q, k, v], axis=-1)

    def gates(w, b):
        g = (qkv @ w + b).astype(jnp.float32)
        return g[..., :H], jax.nn.log_sigmoid(g[..., H:])

    qh = q.reshape(B, S, H, Dh).astype(jnp.float32)
    kh = k.reshape(B, S, H, Dh).astype(jnp.float32)
    vh = v.reshape(B, S, H, Dh).astype(jnp.float32)
    ig_f, lf_f = gates(gate_w_fwd, gate_b_fwd)
    ig_b, lf_b = gates(gate_w_bwd, gate_b_bwd)
    h_fwd = mlstm_chunkwise(qh, kh, vh, ig_f, lf_f)
    flip = lambda t: jnp.flip(t, axis=1)
    h_bwd = flip(mlstm_chunkwise(flip(qh), flip(kh), flip(vh), flip(ig_b), flip(lf_b)))
    hn = head_layer_norm(h_fwd + h_bwd, mh_norm).reshape(B, S, MLSTM_WIDTH).astype(xm.dtype)
    return (hn + skip * xc) * jax.nn.sigmoid(og)


def setup_inputs(seed: int = 0) -> dict:
    key = jax.random.key(seed)
    ks = iter(jax.random.split(key, 40))
    nrm = lambda shape, scale: jax.random.normal(next(ks), shape, jnp.float32) * scale
    gain = lambda shape: 1.0 + nrm(shape, 0.02)
    H = MLSTM_HEADS

    def gate_bias():
        ib = nrm((DEPTH, H), 0.1)
        fb = jnp.linspace(3.0, 6.0, H, dtype=jnp.float32)[None, :] + nrm((DEPTH, H), 0.01)
        return jnp.concatenate([ib, fb], axis=-1)

    return {
        'x': nrm((BATCH, SEQ, D_MODEL), 1.0),
        'ffn1_norm': gain((DEPTH, D_MODEL)),
        'ffn1_w_gate': nrm((DEPTH, D_MODEL, FFN_HIDDEN), D_MODEL ** -0.5),
        'ffn1_w_up': nrm((DEPTH, D_MODEL, FFN_HIDDEN), D_MODEL ** -0.5),
        'ffn1_w_down': nrm((DEPTH, FFN_HIDDEN, D_MODEL), FFN_HIDDEN ** -0.5),
        'mix_norm': gain((DEPTH, D_MODEL)),
        'w_in': nrm((DEPTH, D_MODEL, IN_WIDTH), D_MODEL ** -0.5),
        'sgu_norm': gain((DEPTH, SGU_HEADS, SGU_HEAD_DIM)),
        'sgu_w': nrm((DEPTH, SGU_HEADS, SGU_CHUNK, SGU_CHUNK), 0.5 * SGU_CHUNK ** -0.5),
        'sgu_b': 1.0 + nrm((DEPTH, SGU_HEADS, SGU_CHUNK), 0.1),
        'conv_w': nrm((DEPTH, CONV_WIDTH, MLSTM_WIDTH), CONV_WIDTH ** -0.5),
        'conv_b': nrm((DEPTH, MLSTM_WIDTH), 0.02),
        'w_q': nrm((DEPTH, MLSTM_WIDTH // QKV_BLOCK, QKV_BLOCK, QKV_BLOCK), QKV_BLOCK ** -0.5),
        'w_k': nrm((DEPTH, MLSTM_WIDTH // QKV_BLOCK, QKV_BLOCK, QKV_BLOCK), QKV_BLOCK ** -0.5),
        'w_v': nrm((DEPTH, MLSTM_WIDTH // QKV_BLOCK, QKV_BLOCK, QKV_BLOCK), QKV_BLOCK ** -0.5),
        'gate_w_fwd': nrm((DEPTH, 3 * MLSTM_WIDTH, 2 * H), 0.1 * (3 * MLSTM_WIDTH) ** -0.5),
        'gate_b_fwd': gate_bias(),
        'gate_w_bwd': nrm((DEPTH, 3 * MLSTM_WIDTH, 2 * H), 0.1 * (3 * MLSTM_WIDTH) ** -0.5),
        'gate_b_bwd': gate_bias(),
        'mh_norm': gain((DEPTH, MLSTM_HEADS, MLSTM_HEAD_DIM)),
        'mlstm_skip': gain((DEPTH, MLSTM_WIDTH)),
        'w_out': nrm((DEPTH, MIX_WIDTH, D_MODEL), MIX_WIDTH ** -0.5),
        'ffn2_norm': gain((DEPTH, D_MODEL)),
        'ffn2_w_gate': nrm((DEPTH, D_MODEL, FFN_HIDDEN), D_MODEL ** -0.5),
        'ffn2_w_up': nrm((DEPTH, D_MODEL, FFN_HIDDEN), D_MODEL ** -0.5),
        'ffn2_w_down': nrm((DEPTH, FFN_HIDDEN, D_MODEL), FFN_HIDDEN ** -0.5),
        'final_norm': gain((D_MODEL,)),
    }


def reference(x, ffn1_norm, ffn1_w_gate, ffn1_w_up, ffn1_w_down, mix_norm, w_in,
              sgu_norm, sgu_w, sgu_b, conv_w, conv_b, w_q, w_k, w_v,
              gate_w_fwd, gate_b_fwd, gate_w_bwd, gate_b_bwd, mh_norm, mlstm_skip, w_out,
              ffn2_norm, ffn2_w_gate, ffn2_w_up, ffn2_w_down, final_norm):
    split_at = [SGU_WIDTH, 2 * SGU_WIDTH, 2 * SGU_WIDTH + MLSTM_WIDTH]
    for l in range(DEPTH):
        h = x + 0.5 * swiglu(rms_norm(x, ffn1_norm[l]), ffn1_w_gate[l], ffn1_w_up[l], ffn1_w_down[l])
        z = rms_norm(h, mix_norm[l]) @ w_in[l]
        u, v, xm, og = jnp.split(z, split_at, axis=-1)
        y_sgu = spatial_gating(u, v, sgu_norm[l], sgu_w[l], sgu_b[l])
        y_mlstm = mlstm_mixer(xm, og, conv_w[l], conv_b[l], w_q[l], w_k[l], w_v[l],
                              gate_w_fwd[l], gate_b_fwd[l], gate_w_bwd[l], gate_b_bwd[l],
                              mh_norm[l], mlstm_skip[l])
        h = h + jnp.concatenate([y_sgu, y_mlstm], axis=-1) @ w_out[l]
        x = h + 0.5 * swiglu(rms_norm(h, ffn2_norm[l]), ffn2_w_gate[l], ffn2_w_up[l], ffn2_w_down[l])
    return rms_norm(x, final_norm)
```

```python
import functools

import jax
import jax.numpy as jnp
from jax import lax
from jax.experimental import pallas as pl
from jax.experimental.pallas import tpu as pltpu

F32 = jnp.float32
BF16 = jnp.bfloat16

EPS = 1e-6
HEADS = 4
HEAD_DIM = 128
CHUNK = 128
GROUP_WIDTH = HEADS * HEAD_DIM
CONV_WIDTH = 5
QKV_BLOCK = 4
GATE_LANES = 128
HALO_ROWS = 8

FFN_TILE = 512
MIX_TILE = 512
FFN_HIDDEN_CHUNK = 1024
VMEM_LIMIT_BYTES = 56 * 1024 * 1024

_NT = (((1,), (1,)), ((), ()))
_TN = (((0,), (0,)), ((), ()))


def _rms_norm(x, g):
    return x * lax.rsqrt(jnp.mean(x * x, axis=-1, keepdims=True) + EPS) * g


def _layer_norm(x, g):
    mu = jnp.mean(x, axis=-1, keepdims=True)
    xc = x - mu
    var = jnp.mean(xc * xc, axis=-1, keepdims=True)
    return xc * lax.rsqrt(var + EPS) * g


def _log_sigmoid(x):
    return jnp.minimum(x, 0.0) - jnp.log1p(jnp.exp(-jnp.abs(x)))


def _swiglu(xn, wg_ref, wu_ref, wd_ref):
    hidden = wg_ref.shape[1]
    acc = None
    for lo in range(0, hidden, FFN_HIDDEN_CHUNK):
        hi = min(lo + FFN_HIDDEN_CHUNK, hidden)
        gate = jnp.dot(xn, wg_ref[:, lo:hi], preferred_element_type=F32)
        up = jnp.dot(xn, wu_ref[:, lo:hi], preferred_element_type=F32)
        act = (jax.nn.silu(gate) * up).astype(BF16)
        part = jnp.dot(act, wd_ref[lo:hi, :], preferred_element_type=F32)
        acc = part if acc is None else acc + part
    return acc


def _ffn_in_kernel(x_ref, g1_ref, wg_ref, wu_ref, wd_ref, g2_ref, win_ref, h_ref, z_ref):
    x = x_ref[...]
    xn = _rms_norm(x, g1_ref[...]).astype(BF16)
    h = x + 0.5 * _swiglu(xn, wg_ref, wu_ref, wd_ref)
    h_ref[...] = h
    hn = _rms_norm(h, g2_ref[...]).astype(BF16)
    z_ref[...] = jnp.dot(hn, win_ref[...], preferred_element_type=F32)


def _ffn_out_kernel(final_norm, h_ref, g1_ref, wg_ref, wu_ref, wd_ref, gf_ref, o_ref):
    h = h_ref[...]
    hn = _rms_norm(h, g1_ref[...]).astype(BF16)
    x = h + 0.5 * _swiglu(hn, wg_ref, wu_ref, wd_ref)
    if final_norm:
        x = _rms_norm(x, gf_ref[...])
    o_ref[...] = x


def _mlstm_prepare(reverse, first, last, xm_ref, xprev_ref, xnext_ref, convw_ref, convb_ref,
                   bdq_ref, bdk_ref, bdv_ref, wgate_ref, bgate_ref,
                   q_s, k_s, v_s, gc_s, gr_s):
    tile = xm_ref.shape[0]
    xm = xm_ref[...]
    prev = jnp.where(first, 0.0, xprev_ref[...])
    nxt = jnp.where(last, 0.0, xnext_ref[...])
    ext = jnp.concatenate([prev, xm, nxt], axis=0)
    n_ext = tile + 2 * HALO_ROWS
    conv = None
    for tap in range(CONV_WIDTH):
        off = tap - CONV_WIDTH // 2
        shifted = ext if off == 0 else pltpu.roll(ext, (n_ext - off) % n_ext, axis=0)
        term = shifted[HALO_ROWS:HALO_ROWS + tile] * convw_ref[tap:tap + 1, :]
        conv = term if conv is None else conv + term
    xc = jax.nn.silu(conv + convb_ref[...])

    xc_b = xc.astype(BF16)
    q = jnp.dot(xc_b, bdq_ref[...], preferred_element_type=F32).astype(BF16)
    k = (jnp.dot(xc_b, bdk_ref[...], preferred_element_type=F32) * (HEAD_DIM ** -0.5)).astype(BF16)
    v = jnp.dot(xm.astype(BF16), bdv_ref[...], preferred_element_type=F32)
    q_s[...] = q
    k_s[...] = k
    v_s[...] = v

    gates = (jnp.dot(q, wgate_ref[0:GROUP_WIDTH, :], preferred_element_type=F32)
             + jnp.dot(k, wgate_ref[GROUP_WIDTH:2 * GROUP_WIDTH, :], preferred_element_type=F32)
             + jnp.dot(v.astype(BF16), wgate_ref[2 * GROUP_WIDTH:, :], preferred_element_type=F32)
             + bgate_ref[...])
    logf = _log_sigmoid(gates)
    lf_hi = logf.astype(BF16)
    rem = logf - lf_hi.astype(F32)
    lf_mid = rem.astype(BF16)
    lf_lo = (rem - lf_mid.astype(F32)).astype(BF16)

    row = lax.broadcasted_iota(jnp.int32, (CHUNK, CHUNK), 0)
    col = lax.broadcasted_iota(jnp.int32, (CHUNK, CHUNK), 1)
    visible = (col >= row) if reverse else (col <= row)
    prefix = visible.astype(BF16)
    lane = lax.broadcasted_iota(jnp.int32, (CHUNK, GATE_LANES), 1)
    is_forget = (lane >= HEADS) & (lane < 2 * HEADS)
    for c in range(tile // CHUNK):
        rows = slice(c * CHUNK, (c + 1) * CHUNK)
        cum = (jnp.dot(prefix, lf_hi[rows], preferred_element_type=F32)
               + jnp.dot(prefix, lf_mid[rows], preferred_element_type=F32)
               + jnp.dot(prefix, lf_lo[rows], preferred_element_type=F32))
        gc = jnp.where(is_forget, cum, gates[rows])
        gc_s[rows, :] = gc
        gr_s[rows, :] = gc.T
    return xc


def _mlstm_direction(reverse, n_chunks, q_s, k_s, v_s, gc_s, gr_s, cext_s, m_s, out_ref):
    row = lax.broadcasted_iota(jnp.int32, (CHUNK, CHUNK), 0)
    col = lax.broadcasted_iota(jnp.int32, (CHUNK, CHUNK), 1)
    visible = (col >= row) if reverse else (col <= row)
    lane = lax.broadcasted_iota(jnp.int32, (CHUNK, HEAD_DIM), 1)
    edge = 0 if reverse else CHUNK - 1

    def chunk_body(i, carry):
        c = (n_chunks - 1 - i) if reverse else i
        r0 = pl.multiple_of(c * CHUNK, CHUNK)
        gc = gc_s[pl.ds(r0, CHUNK), :]
        gr = gr_s[pl.ds(r0, CHUNK), :]
        for h in range(HEADS):
            cols = slice(h * HEAD_DIM, (h + 1) * HEAD_DIM)
            qh = q_s[pl.ds(r0, CHUNK), cols]
            kh = k_s[pl.ds(r0, CHUNK), cols]
            vh = v_s[pl.ds(r0, CHUNK), cols]
            ig_row = gr[h:h + 1, :]
            b_row = gr[HEADS + h:HEADS + h + 1, :]
            ig_col = gc[:, h:h + 1]
            b_col = gc[:, HEADS + h:HEADS + h + 1]
            m_prev = m_s[h:h + 1, 0:1]

            d = jnp.where(visible, (b_col - b_row) + ig_row, -jnp.inf)
            inter = b_col + m_prev
            m_j = jnp.maximum(inter, jnp.max(d, axis=-1, keepdims=True))
            w_intra = jnp.exp(d - m_j)
            w_inter = jnp.exp(inter - m_j)
            s = lax.dot_general(qh, kh, _NT, preferred_element_type=F32) * w_intra
            cext = cext_s[h]
            carried = jnp.dot(qh, cext.astype(BF16), preferred_element_type=F32)
            num = (w_inter * carried[:, :HEAD_DIM]
                   + jnp.dot(s.astype(BF16), vh.astype(BF16), preferred_element_type=F32))
            nq = w_inter * carried[:, HEAD_DIM:HEAD_DIM + 1] + jnp.sum(s, axis=-1, keepdims=True)
            out_ref[pl.ds(r0, CHUNK), cols] = num / jnp.maximum(jnp.abs(nq), jnp.exp(-m_j))

            b_edge = b_col[edge:edge + 1, :]
            dec = (b_edge - b_col) + ig_col
            m_new = jnp.maximum(b_edge + m_prev, jnp.max(dec, axis=0, keepdims=True))
            wk = jnp.exp(dec - m_new)
            scale = jnp.exp(b_edge + m_prev - m_new)
            rhs = jnp.concatenate([wk * vh, jnp.where(lane == 0, wk, 0.0)], axis=1).astype(BF16)
            cext_s[h] = scale * cext + lax.dot_general(kh, rhs, _TN, preferred_element_type=F32)
            m_s[h:h + 1, :] = jnp.broadcast_to(m_new, (1, m_s.shape[1]))
        return carry

    lax.fori_loop(0, n_chunks, chunk_body, 0)


def _reset_state(cext_s, m_s):
    cext_s[...] = jnp.zeros_like(cext_s)
    m_s[...] = jnp.zeros_like(m_s)


def _mix_bwd_kernel(n_tiles, xm_ref, xprev_ref, xnext_ref, convw_ref, convb_ref, bdq_ref, bdk_ref,
                    bdv_ref, wgate_ref, bgate_ref, hb_ref,
                    q_s, k_s, v_s, gc_s, gr_s, cext_s, m_s):
    j = pl.program_id(1)
    seq_tile = n_tiles - 1 - j

    @pl.when(j == 0)
    def _():
        _reset_state(cext_s, m_s)

    _mlstm_prepare(True, seq_tile == 0, seq_tile == n_tiles - 1, xm_ref, xprev_ref, xnext_ref,
                   convw_ref, convb_ref, bdq_ref, bdk_ref, bdv_ref, wgate_ref, bgate_ref,
                   q_s, k_s, v_s, gc_s, gr_s)
    _mlstm_direction(True, xm_ref.shape[0] // CHUNK, q_s, k_s, v_s, gc_s, gr_s, cext_s, m_s, hb_ref)


def _mix_fwd_kernel(n_tiles, u_ref, v_ref, xm_ref, xprev_ref, xnext_ref, og_ref, h_ref, hb_ref,
                    convw_ref, convb_ref, bdq_ref, bdk_ref, bdv_ref, wgate_ref, bgate_ref,
                    sgun_ref, sguw_ref, sgub_ref, mhn_ref, skip_ref, wout_ref, o_ref,
                    q_s, k_s, v_s, gc_s, gr_s, cext_s, m_s, hf_s, y_s):
    j = pl.program_id(1)
    tile = xm_ref.shape[0]

    @pl.when(j == 0)
    def _():
        _reset_state(cext_s, m_s)

    xc = _mlstm_prepare(False, j == 0, j == n_tiles - 1, xm_ref, xprev_ref, xnext_ref,
                        convw_ref, convb_ref, bdq_ref, bdk_ref, bdv_ref, wgate_ref, bgate_ref,
                        q_s, k_s, v_s, gc_s, gr_s)
    _mlstm_direction(False, tile // CHUNK, q_s, k_s, v_s, gc_s, gr_s, cext_s, m_s, hf_s)

    for h in range(HEADS):
        cols = slice(h * HEAD_DIM, (h + 1) * HEAD_DIM)
        vn = _layer_norm(jax.nn.gelu(v_ref[:, cols]), sgun_ref[h:h + 1, :]).astype(BF16)
        ug = jax.nn.gelu(u_ref[:, cols])
        for c in range(tile // CHUNK):
            rows = slice(c * CHUNK, (c + 1) * CHUNK)
            mixed = jnp.dot(sguw_ref[h], vn[rows], preferred_element_type=F32) + sgub_ref[h]
            y_s[rows, cols] = (ug[rows] * mixed).astype(BF16)

    for h in range(HEADS):
        cols = slice(h * HEAD_DIM, (h + 1) * HEAD_DIM)
        hn = _layer_norm(hf_s[:, cols] + hb_ref[:, cols], mhn_ref[h:h + 1, :])
        ym = (hn + skip_ref[:, cols] * xc[:, cols]) * jax.nn.sigmoid(og_ref[:, cols])
        y_s[:, GROUP_WIDTH + h * HEAD_DIM:GROUP_WIDTH + (h + 1) * HEAD_DIM] = ym.astype(BF16)

    o_ref[...] = h_ref[...] + jnp.dot(y_s[...], wout_ref[...], preferred_element_type=F32)


def _resident(shape, layer=None):
    nd = len(shape)
    if layer is None:
        return pl.BlockSpec(shape, lambda *_: (0,) * nd, pipeline_mode=pl.Buffered(1))
    return pl.BlockSpec((None,) + shape, lambda *_: (layer,) + (0,) * nd,
                        pipeline_mode=pl.Buffered(1))


def _ffn_in(x, l, g1, wg, wu, wd, g2, win):
    tokens, d = x.shape
    hidden = wg.shape[-1]
    zw = win.shape[-1]
    row = lambda i: (i, 0)
    return pl.pallas_call(
        _ffn_in_kernel,
        grid=(tokens // FFN_TILE,),
        in_specs=[pl.BlockSpec((FFN_TILE, d), row),
                  _resident((1, d), l), _resident((d, hidden), l), _resident((d, hidden), l),
                  _resident((hidden, d), l), _resident((1, d), l), _resident((d, zw), l)],
        out_specs=[pl.BlockSpec((FFN_TILE, d), row), pl.BlockSpec((FFN_TILE, zw), row)],
        out_shape=[jax.ShapeDtypeStruct((tokens, d), F32), jax.ShapeDtypeStruct((tokens, zw), F32)],
        compiler_params=pltpu.CompilerParams(dimension_semantics=("arbitrary",),
                                             vmem_limit_bytes=VMEM_LIMIT_BYTES),
        name=f"ffn_in_{l}",
    )(x, g1, wg, wu, wd, g2, win)


def _ffn_out(h, l, g1, wg, wu, wd, gf, final_norm):
    tokens, d = h.shape
    hidden = wg.shape[-1]
    row = lambda i: (i, 0)
    return pl.pallas_call(
        functools.partial(_ffn_out_kernel, final_norm),
        grid=(tokens // FFN_TILE,),
        in_specs=[pl.BlockSpec((FFN_TILE, d), row),
                  _resident((1, d), l), _resident((d, hidden), l), _resident((d, hidden), l),
                  _resident((hidden, d), l), _resident((1, d))],
        out_specs=pl.BlockSpec((FFN_TILE, d), row),
        out_shape=jax.ShapeDtypeStruct((tokens, d), F32),
        compiler_params=pltpu.CompilerParams(dimension_semantics=("arbitrary",),
                                             vmem_limit_bytes=VMEM_LIMIT_BYTES),
        name=f"ffn_out_{l}",
    )(h, g1, wg, wu, wd, gf)


def _mixer_scratch():
    return [pltpu.VMEM((MIX_TILE, GROUP_WIDTH), BF16),
            pltpu.VMEM((MIX_TILE, GROUP_WIDTH), BF16),
            pltpu.VMEM((MIX_TILE, GROUP_WIDTH), F32),
            pltpu.VMEM((MIX_TILE, GATE_LANES), F32),
            pltpu.VMEM((MIX_TILE, CHUNK), F32),
            pltpu.VMEM((HEADS, HEAD_DIM, 2 * HEAD_DIM), F32),
            pltpu.VMEM((8, 128), F32)]


def _tile_maps(n_tiles, reverse):
    halo_per_tile = MIX_TILE // HALO_ROWS

    def tile_row(b, j):
        return b * n_tiles + ((n_tiles - 1 - j) if reverse else j)

    def group(g):
        return lambda b, j: (tile_row(b, j), g)

    def prev_halo(b, j):
        return (jnp.maximum(tile_row(b, j) * halo_per_tile - 1, 0), 2)

    def next_halo(total_halo_blocks):
        return lambda b, j: (jnp.minimum((tile_row(b, j) + 1) * halo_per_tile,
                                         total_halo_blocks - 1), 2)

    return group, prev_halo, next_halo


def _mlstm_weight_specs(l):
    return [_resident((CONV_WIDTH, GROUP_WIDTH), l), _resident((1, GROUP_WIDTH), l),
            _resident((GROUP_WIDTH, GROUP_WIDTH), l), _resident((GROUP_WIDTH, GROUP_WIDTH), l),
            _resident((GROUP_WIDTH, GROUP_WIDTH), l),
            _resident((3 * GROUP_WIDTH, GATE_LANES), l), _resident((1, GATE_LANES), l)]


def _mix_bwd(z, l, batch, mw):
    tokens = z.shape[0]
    n_tiles = tokens // batch // MIX_TILE
    group, prev_halo, next_halo = _tile_maps(n_tiles, True)
    gw = GROUP_WIDTH
    return pl.pallas_call(
        functools.partial(_mix_bwd_kernel, n_tiles),
        grid=(batch, n_tiles),
        in_specs=[pl.BlockSpec((MIX_TILE, gw), group(2)),
                  pl.BlockSpec((HALO_ROWS, gw), prev_halo),
                  pl.BlockSpec((HALO_ROWS, gw), next_halo(tokens // HALO_ROWS))]
                 + _mlstm_weight_specs(l),
        out_specs=pl.BlockSpec((MIX_TILE, gw), group(0)),
        out_shape=jax.ShapeDtypeStruct((tokens, gw), F32),
        scratch_shapes=_mixer_scratch(),
        compiler_params=pltpu.CompilerParams(dimension_semantics=("arbitrary", "arbitrary"),
                                             vmem_limit_bytes=VMEM_LIMIT_BYTES),
        name=f"mix_bwd_{l}",
    )(z, z, z, *mw)


def _mix_fwd(z, h, hb, l, batch, mw, sgun, sguw, sgub, mhn, skip, wout):
    tokens, d = h.shape
    n_tiles = tokens // batch // MIX_TILE
    group, prev_halo, next_halo = _tile_maps(n_tiles, False)
    gw = GROUP_WIDTH
    row = group(0)
    return pl.pallas_call(
        functools.partial(_mix_fwd_kernel, n_tiles),
        grid=(batch, n_tiles),
        in_specs=[pl.BlockSpec((MIX_TILE, gw), group(0)),
                  pl.BlockSpec((MIX_TILE, gw), group(1)),
                  pl.BlockSpec((MIX_TILE, gw), group(2)),
                  pl.BlockSpec((HALO_ROWS, gw), prev_halo),
                  pl.BlockSpec((HALO_ROWS, gw), next_halo(tokens // HALO_ROWS)),
                  pl.BlockSpec((MIX_TILE, gw), group(3)),
                  pl.BlockSpec((MIX_TILE, d), row),
                  pl.BlockSpec((MIX_TILE, gw), row)]
                 + _mlstm_weight_specs(l)
                 + [_resident((HEADS, HEAD_DIM), l), _resident((HEADS, CHUNK, CHUNK), l),
                    _resident((HEADS, CHUNK, HEAD_DIM), l), _resident((HEADS, HEAD_DIM), l),
                    _resident((1, gw), l), _resident((2 * gw, d), l)],
        out_specs=pl.BlockSpec((MIX_TILE, d), row),
        out_shape=jax.ShapeDtypeStruct((tokens, d), F32),
        scratch_shapes=_mixer_scratch() + [pltpu.VMEM((MIX_TILE, gw), F32),
                                           pltpu.VMEM((MIX_TILE, 2 * gw), BF16)],
        compiler_params=pltpu.CompilerParams(dimension_semantics=("arbitrary", "arbitrary"),
                                             vmem_limit_bytes=VMEM_LIMIT_BYTES),
        name=f"mix_fwd_{l}",
    )(z, z, z, z, z, z, h, hb, *mw, sgun, sguw, sgub, mhn, skip, wout)


def _block_diag(w):
    depth, groups, bi, bo = w.shape
    eye = jnp.eye(groups, dtype=w.dtype)
    dense = jnp.einsum('lgio,gh->lgiho', w, eye)
    return dense.reshape(depth, groups * bi, groups * bo).astype(BF16)


def _pad_gates(w, b):
    depth = w.shape[0]
    wp = jnp.zeros((depth, w.shape[1], GATE_LANES), F32).at[:, :, :w.shape[2]].set(w).astype(BF16)
    bp = jnp.zeros((depth, 1, GATE_LANES), F32).at[:, 0, :b.shape[1]].set(b)
    return wp, bp


def kernel(x, ffn1_norm, ffn1_w_gate, ffn1_w_up, ffn1_w_down, mix_norm, w_in, sgu_norm, sgu_w, sgu_b, conv_w, conv_b, w_q, w_k, w_v, gate_w_fwd, gate_b_fwd, gate_w_bwd, gate_b_bwd, mh_norm, mlstm_skip, w_out, ffn2_norm, ffn2_w_gate, ffn2_w_up, ffn2_w_down, final_norm):
    batch, seq, d = x.shape
    depth = w_in.shape[0]
    assert seq % MIX_TILE == 0 and (batch * seq) % FFN_TILE == 0
    assert sgu_w.shape[1:] == (HEADS, CHUNK, CHUNK) and mh_norm.shape[1:] == (HEADS, HEAD_DIM)

    bf = lambda w: w.astype(BF16)
    row3 = lambda g: g.reshape(depth, 1, -1)
    f1 = (row3(ffn1_norm), bf(ffn1_w_gate), bf(ffn1_w_up), bf(ffn1_w_down))
    f2 = (row3(ffn2_norm), bf(ffn2_w_gate), bf(ffn2_w_up), bf(ffn2_w_down))
    bdq, bdk, bdv = _block_diag(w_q), _block_diag(w_k), _block_diag(w_v)
    wgf, bgf = _pad_gates(gate_w_fwd, gate_b_fwd)
    wgb, bgb = _pad_gates(gate_w_bwd, gate_b_bwd)
    convb = row3(conv_b)
    sgub = jnp.broadcast_to(sgu_b[..., None], sgu_b.shape + (HEAD_DIM,))
    gfin = final_norm.reshape(1, d)

    xt = x.reshape(batch * seq, d)
    for l in range(depth):
        h, z = _ffn_in(xt, l, *f1, row3(mix_norm), bf(w_in))
        hb = _mix_bwd(z, l, batch, (conv_w, convb, bdq, bdk, bdv, wgb, bgb))
        h2 = _mix_fwd(z, h, hb, l, batch, (conv_w, convb, bdq, bdk, bdv, wgf, bgf),
                      sgu_norm, bf(sgu_w), sgub, mh_norm, row3(mlstm_skip), bf(w_out))
        xt = _ffn_out(h2, l, *f2, gfin, final_norm=(l == depth - 1))
    return xt.reshape(batch, seq, d)
```

```python
import functools

import jax
import jax.numpy as jnp
from jax import lax
from jax.experimental import pallas as pl
from jax.experimental.pallas import tpu as pltpu

F32 = jnp.float32
BF16 = jnp.bfloat16

EPS = 1e-6
HEADS = 4
HEAD_DIM = 128
CHUNK = 128
GROUP_WIDTH = HEADS * HEAD_DIM
CONV_WIDTH = 5
QKV_BLOCK = 4
MXU_WIDTH = 256
GATE_LANES = 128
IG_LANE = {False: 0, True: 2 * HEADS}
B_LANE = {False: HEADS, True: 3 * HEADS}
STATE_ROWS = HEAD_DIM + 16
HALO_ROWS = 8

FFN_TILE = 512
MIX_TILE = 512
FFN_HIDDEN_CHUNK = 1024
VMEM_LIMIT_BYTES = 56 * 1024 * 1024

_NT = (((1,), (1,)), ((), ()))


def _rms_norm(x, g):
    return x * lax.rsqrt(jnp.mean(x * x, axis=-1, keepdims=True) + EPS) * g


def _layer_norm(x, g):
    mu = jnp.mean(x, axis=-1, keepdims=True)
    xc = x - mu
    var = jnp.mean(xc * xc, axis=-1, keepdims=True)
    return xc * lax.rsqrt(var + EPS) * g


def _log_sigmoid(x):
    return jnp.minimum(x, 0.0) - jnp.log1p(jnp.exp(-jnp.abs(x)))


def _swiglu(xn, wg_ref, wu_ref, wd_ref):
    hidden = wg_ref.shape[1]
    acc = None
    for lo in range(0, hidden, FFN_HIDDEN_CHUNK):
        hi = min(lo + FFN_HIDDEN_CHUNK, hidden)
        gate = jnp.dot(xn, wg_ref[:, lo:hi], preferred_element_type=F32)
        up = jnp.dot(xn, wu_ref[:, lo:hi], preferred_element_type=F32)
        act = (jax.nn.silu(gate) * up).astype(BF16)
        part = jnp.dot(act, wd_ref[lo:hi, :], preferred_element_type=F32)
        acc = part if acc is None else acc + part
    return acc


def _ffn_in_kernel(x_ref, g1_ref, wg_ref, wu_ref, wd_ref, g2_ref, win_ref, h_ref, z_ref):
    x = x_ref[...]
    xn = _rms_norm(x, g1_ref[...]).astype(BF16)
    h = x + 0.5 * _swiglu(xn, wg_ref, wu_ref, wd_ref)
    h_ref[...] = h
    hn = _rms_norm(h, g2_ref[...]).astype(BF16)
    z_ref[...] = jnp.dot(hn, win_ref[...], preferred_element_type=F32)


def _ffn_out_kernel(final_norm, h_ref, g1_ref, wg_ref, wu_ref, wd_ref, gf_ref, o_ref):
    h = h_ref[...]
    hn = _rms_norm(h, g1_ref[...]).astype(BF16)
    x = h + 0.5 * _swiglu(hn, wg_ref, wu_ref, wd_ref)
    if final_norm:
        x = _rms_norm(x, gf_ref[...])
    o_ref[...] = x


def _block_diag_dot(x, bd_ref):
    parts = [jnp.dot(x[:, i * MXU_WIDTH:(i + 1) * MXU_WIDTH], bd_ref[i], preferred_element_type=F32)
             for i in range(bd_ref.shape[0])]
    return jnp.concatenate(parts, axis=1)


def _mlstm_prepare(first, last, xm_ref, xprev_ref, xnext_ref, convw_ref, convb_ref,
                   bdq_ref, bdk_ref, bdv_ref, bdvt_ref, wgate_ref, bgate_ref,
                   xc_ref, q_ref, k_ref, vt_ref, ga_ref, gr_ref):
    tile = xm_ref.shape[0]
    xm = xm_ref[...]
    prev = jnp.where(first, 0.0, xprev_ref[...])
    nxt = jnp.where(last, 0.0, xnext_ref[...])
    ext = jnp.concatenate([prev, xm, nxt], axis=0)
    n_ext = tile + 2 * HALO_ROWS
    conv = None
    for tap in range(CONV_WIDTH):
        off = tap - CONV_WIDTH // 2
        shifted = ext if off == 0 else pltpu.roll(ext, (n_ext - off) % n_ext, axis=0)
        term = shifted[HALO_ROWS:HALO_ROWS + tile] * convw_ref[tap:tap + 1, :]
        conv = term if conv is None else conv + term
    xc = jax.nn.silu(conv + convb_ref[...])
    xc_ref[...] = xc

    xc_b = xc.astype(BF16)
    xm_b = xm.astype(BF16)
    q = _block_diag_dot(xc_b, bdq_ref).astype(BF16)
    k = (_block_diag_dot(xc_b, bdk_ref) * (HEAD_DIM ** -0.5)).astype(BF16)
    v = _block_diag_dot(xm_b, bdv_ref).astype(BF16)
    q_ref[...] = q
    k_ref[...] = k
    for i in range(bdvt_ref.shape[0]):
        vt = lax.dot_general(bdvt_ref[i], xm_b[:, i * MXU_WIDTH:(i + 1) * MXU_WIDTH], _NT,
                             preferred_element_type=F32)
        for c in range(tile // CHUNK):
            vt_ref[c, i * MXU_WIDTH:(i + 1) * MXU_WIDTH, :] = vt[:, c * CHUNK:(c + 1) * CHUNK]

    gates = (jnp.dot(q, wgate_ref[0:GROUP_WIDTH, :], preferred_element_type=F32)
             + jnp.dot(k, wgate_ref[GROUP_WIDTH:2 * GROUP_WIDTH, :], preferred_element_type=F32)
             + jnp.dot(v, wgate_ref[2 * GROUP_WIDTH:, :], preferred_element_type=F32)
             + bgate_ref[...])
    logf = _log_sigmoid(gates)
    lf_hi = logf.astype(BF16)
    rem = logf - lf_hi.astype(F32)
    lf_mid = rem.astype(BF16)
    lf_lo = (rem - lf_mid.astype(F32)).astype(BF16)

    row = lax.broadcasted_iota(jnp.int32, (CHUNK, CHUNK), 0)
    col = lax.broadcasted_iota(jnp.int32, (CHUNK, CHUNK), 1)
    prefix_fwd = (col <= row).astype(BF16)
    prefix_bwd = (col >= row).astype(BF16)
    lane = lax.broadcasted_iota(jnp.int32, (CHUNK, GATE_LANES), 1)
    is_b_fwd = (lane >= B_LANE[False]) & (lane < B_LANE[False] + HEADS)
    is_b_bwd = (lane >= B_LANE[True]) & (lane < B_LANE[True] + HEADS)

    def prefix_sum(prefix, rows):
        return (jnp.dot(prefix, lf_hi[rows], preferred_element_type=F32)
                + jnp.dot(prefix, lf_mid[rows], preferred_element_type=F32)
                + jnp.dot(prefix, lf_lo[rows], preferred_element_type=F32))

    for c in range(tile // CHUNK):
        rows = slice(c * CHUNK, (c + 1) * CHUNK)
        gc = jnp.where(is_b_fwd, prefix_sum(prefix_fwd, rows),
                       jnp.where(is_b_bwd, prefix_sum(prefix_bwd, rows), gates[rows]))
        gr_ref[rows, :] = gc.T
        ga_ref[rows, :] = gc - pltpu.roll(gc, GATE_LANES - HEADS, axis=1)


def _mlstm_direction(reverse, n_chunks, q_ref, k_ref, vt_ref, ga_ref, gr_ref, state_s, m_s, out_ref):
    key = lax.broadcasted_iota(jnp.int32, (CHUNK, CHUNK), 0)
    qry = lax.broadcasted_iota(jnp.int32, (CHUNK, CHUNK), 1)
    visible = (key >= qry) if reverse else (key <= qry)
    first_row = lax.broadcasted_iota(jnp.int32, (STATE_ROWS - HEAD_DIM, CHUNK), 0) == 0
    edge = 0 if reverse else CHUNK - 1
    ig0, b0 = IG_LANE[reverse], B_LANE[reverse]

    def chunk_body(i, carry):
        c = (n_chunks - 1 - i) if reverse else i
        r0 = pl.multiple_of(c * CHUNK, CHUNK)
        ga = ga_ref[pl.ds(r0, CHUNK), :]
        gr = gr_ref[pl.ds(r0, CHUNK), :]
        for h in range(HEADS):
            cols = slice(h * HEAD_DIM, (h + 1) * HEAD_DIM)
            qh = q_ref[pl.ds(r0, CHUNK), cols]
            kh = k_ref[pl.ds(r0, CHUNK), cols]
            vth = vt_ref[c, cols, :]
            ig_row = gr[ig0 + h:ig0 + h + 1, :]
            b_row = gr[b0 + h:b0 + h + 1, :]
            a_col = ga[:, ig0 + h:ig0 + h + 1]
            m_prev = m_s[h:h + 1, :]

            d = jnp.where(visible, b_row + a_col, -jnp.inf)
            m_loc = jnp.max(d, axis=0, keepdims=True)
            st = lax.dot_general(kh, qh, _NT, preferred_element_type=F32) * jnp.exp(d - m_loc)
            sum_loc = jnp.sum(st, axis=0, keepdims=True)
            num_loc = jnp.dot(vth.astype(BF16), st.astype(BF16), preferred_element_type=F32)

            state = state_s[h]
            carried = lax.dot_general(state.astype(BF16), qh, _NT, preferred_element_type=F32)
            inter = b_row + m_prev
            m_j = jnp.maximum(inter, m_loc)
            w_inter = jnp.exp(inter - m_j)
            w_loc = jnp.exp(m_loc - m_j)
            num = w_inter * carried[:HEAD_DIM] + w_loc * num_loc
            nq = w_inter * carried[HEAD_DIM:HEAD_DIM + 1] + w_loc * sum_loc
            out_ref[c, cols, :] = num * (1.0 / jnp.maximum(jnp.abs(nq), jnp.exp(-m_j)))

            b_edge = jnp.broadcast_to(b_row[:, edge:edge + 1], b_row.shape)
            dec = (b_edge - b_row) + ig_row
            dec_max = jnp.broadcast_to(jnp.max(dec, axis=1, keepdims=True), dec.shape)
            wk = jnp.exp(dec - dec_max)
            m_new = jnp.maximum(b_edge + m_prev, dec_max)
            lhs = jnp.concatenate([vth * wk, jnp.where(first_row, wk, 0.0)], axis=0).astype(BF16)
            update = jnp.dot(lhs, kh, preferred_element_type=F32)
            state_s[h] = (jnp.exp(b_edge + m_prev - m_new) * state
                          + jnp.exp(dec_max - m_new) * update)
            m_s[h:h + 1, :] = m_new
        return carry

    lax.fori_loop(0, n_chunks, chunk_body, 0)


def _reset_state(state_s, m_s):
    state_s[...] = jnp.zeros_like(state_s)
    m_s[...] = jnp.zeros_like(m_s)


def _mix_bwd_kernel(n_tiles, xm_ref, xprev_ref, xnext_ref, convw_ref, convb_ref, bdq_ref, bdk_ref,
                    bdv_ref, bdvt_ref, wgate_ref, bgate_ref,
                    hbt_ref, xc_ref, q_ref, k_ref, vt_ref, ga_ref, gr_ref, state_s, m_s):
    j = pl.program_id(1)
    seq_tile = n_tiles - 1 - j

    @pl.when(j == 0)
    def _():
        _reset_state(state_s, m_s)

    _mlstm_prepare(seq_tile == 0, seq_tile == n_tiles - 1, xm_ref, xprev_ref, xnext_ref,
                   convw_ref, convb_ref, bdq_ref, bdk_ref, bdv_ref, bdvt_ref, wgate_ref, bgate_ref,
                   xc_ref, q_ref, k_ref, vt_ref, ga_ref, gr_ref)
    _mlstm_direction(True, xm_ref.shape[0] // CHUNK, q_ref, k_ref, vt_ref, ga_ref, gr_ref,
                     state_s, m_s, hbt_ref)


def _mix_fwd_kernel(u_ref, v_ref, og_ref, h_ref, hbt_ref, xc_ref, q_ref, k_ref, vt_ref, ga_ref, gr_ref,
                    sgun_ref, sguw_ref, sgub_ref, mhn_ref, skip_ref, wout_ref, o_ref,
                    state_s, m_s, hft_s, y_s):
    tile = h_ref.shape[0]

    @pl.when(pl.program_id(1) == 0)
    def _():
        _reset_state(state_s, m_s)

    _mlstm_direction(False, tile // CHUNK, q_ref, k_ref, vt_ref, ga_ref, gr_ref, state_s, m_s, hft_s)

    for h in range(HEADS):
        cols = slice(h * HEAD_DIM, (h + 1) * HEAD_DIM)
        vn = _layer_norm(jax.nn.gelu(v_ref[:, cols]), sgun_ref[h:h + 1, :]).astype(BF16)
        ug = jax.nn.gelu(u_ref[:, cols])
        for c in range(tile // CHUNK):
            rows = slice(c * CHUNK, (c + 1) * CHUNK)
            mixed = jnp.dot(sguw_ref[h], vn[rows], preferred_element_type=F32) + sgub_ref[h]
            y_s[rows, cols] = (ug[rows] * mixed).astype(BF16)

    for c in range(tile // CHUNK):
        rows = slice(c * CHUNK, (c + 1) * CHUNK)
        for h in range(HEADS):
            cols = slice(h * HEAD_DIM, (h + 1) * HEAD_DIM)
            ht = hft_s[c, cols, :] + hbt_ref[c, cols, :]
            mu = jnp.mean(ht, axis=0, keepdims=True)
            hc = ht - mu
            var = jnp.mean(hc * hc, axis=0, keepdims=True)
            hn = (hc * lax.rsqrt(var + EPS) * mhn_ref[h]).T
            ym = (hn + skip_ref[:, cols] * xc_ref[rows, cols]) * jax.nn.sigmoid(og_ref[rows, cols])
            y_s[rows, GROUP_WIDTH + h * HEAD_DIM:GROUP_WIDTH + (h + 1) * HEAD_DIM] = ym.astype(BF16)

    o_ref[...] = h_ref[...] + jnp.dot(y_s[...], wout_ref[...], preferred_element_type=F32)


def _resident(shape, layer=None):
    nd = len(shape)
    if layer is None:
        return pl.BlockSpec(shape, lambda *_: (0,) * nd, pipeline_mode=pl.Buffered(1))
    return pl.BlockSpec((None,) + shape, lambda *_: (layer,) + (0,) * nd,
                        pipeline_mode=pl.Buffered(1))


def _ffn_in(x, l, g1, wg, wu, wd, g2, win):
    tokens, d = x.shape
    hidden = wg.shape[-1]
    zw = win.shape[-1]
    row = lambda i: (i, 0)
    return pl.pallas_call(
        _ffn_in_kernel,
        grid=(tokens // FFN_TILE,),
        in_specs=[pl.BlockSpec((FFN_TILE, d), row),
                  _resident((1, d), l), _resident((d, hidden), l), _resident((d, hidden), l),
                  _resident((hidden, d), l), _resident((1, d), l), _resident((d, zw), l)],
        out_specs=[pl.BlockSpec((FFN_TILE, d), row), pl.BlockSpec((FFN_TILE, zw), row)],
        out_shape=[jax.ShapeDtypeStruct((tokens, d), F32), jax.ShapeDtypeStruct((tokens, zw), F32)],
        compiler_params=pltpu.CompilerParams(dimension_semantics=("arbitrary",),
                                             vmem_limit_bytes=VMEM_LIMIT_BYTES),
        name=f"ffn_in_{l}",
    )(x, g1, wg, wu, wd, g2, win)


def _ffn_out(h, l, g1, wg, wu, wd, gf, final_norm):
    tokens, d = h.shape
    hidden = wg.shape[-1]
    row = lambda i: (i, 0)
    return pl.pallas_call(
        functools.partial(_ffn_out_kernel, final_norm),
        grid=(tokens // FFN_TILE,),
        in_specs=[pl.BlockSpec((FFN_TILE, d), row),
                  _resident((1, d), l), _resident((d, hidden), l), _resident((d, hidden), l),
                  _resident((hidden, d), l), _resident((1, d))],
        out_specs=pl.BlockSpec((FFN_TILE, d), row),
        out_shape=jax.ShapeDtypeStruct((tokens, d), F32),
        compiler_params=pltpu.CompilerParams(dimension_semantics=("arbitrary",),
                                             vmem_limit_bytes=VMEM_LIMIT_BYTES),
        name=f"ffn_out_{l}",
    )(h, g1, wg, wu, wd, gf)


def _state_scratch():
    return [pltpu.VMEM((HEADS, STATE_ROWS, HEAD_DIM), F32),
            pltpu.VMEM((8, 128), F32)]


def _tile_row(n_tiles, reverse):
    return lambda b, j: b * n_tiles + ((n_tiles - 1 - j) if reverse else j)


def _shared_specs(tile_row):
    gw = GROUP_WIDTH
    chunks = MIX_TILE // CHUNK
    rows2 = lambda b, j: (tile_row(b, j), 0)
    rows3 = lambda b, j: (tile_row(b, j), 0, 0)
    return [pl.BlockSpec((MIX_TILE, gw), rows2),
            pl.BlockSpec((MIX_TILE, gw), rows2),
            pl.BlockSpec((MIX_TILE, gw), rows2),
            pl.BlockSpec((chunks, gw, CHUNK), rows3),
            pl.BlockSpec((MIX_TILE, GATE_LANES), rows2),
            pl.BlockSpec((MIX_TILE, CHUNK), rows2)]


def _mix_bwd(z, l, batch, mw):
    tokens = z.shape[0]
    n_tiles = tokens // batch // MIX_TILE
    tile_row = _tile_row(n_tiles, True)
    gw = GROUP_WIDTH
    halo_per_tile = MIX_TILE // HALO_ROWS
    last_halo = tokens // HALO_ROWS - 1
    xm_group = 2
    n_diag = gw // MXU_WIDTH
    chunks = MIX_TILE // CHUNK
    shared_shapes = [jax.ShapeDtypeStruct((tokens, gw), F32),
                     jax.ShapeDtypeStruct((tokens, gw), BF16),
                     jax.ShapeDtypeStruct((tokens, gw), BF16),
                     jax.ShapeDtypeStruct((tokens // CHUNK, gw, CHUNK), F32),
                     jax.ShapeDtypeStruct((tokens, GATE_LANES), F32),
                     jax.ShapeDtypeStruct((tokens, CHUNK), F32)]
    return pl.pallas_call(
        functools.partial(_mix_bwd_kernel, n_tiles),
        grid=(batch, n_tiles),
        in_specs=[pl.BlockSpec((MIX_TILE, gw), lambda b, j: (tile_row(b, j), xm_group)),
                  pl.BlockSpec((HALO_ROWS, gw), lambda b, j: (
                      jnp.maximum(tile_row(b, j) * halo_per_tile - 1, 0), xm_group)),
                  pl.BlockSpec((HALO_ROWS, gw), lambda b, j: (
                      jnp.minimum((tile_row(b, j) + 1) * halo_per_tile, last_halo), xm_group)),
                  _resident((CONV_WIDTH, gw), l), _resident((1, gw), l),
                  _resident((n_diag, MXU_WIDTH, MXU_WIDTH), l),
                  _resident((n_diag, MXU_WIDTH, MXU_WIDTH), l),
                  _resident((n_diag, MXU_WIDTH, MXU_WIDTH), l),
                  _resident((n_diag, MXU_WIDTH, MXU_WIDTH), l),
                  _resident((3 * gw, GATE_LANES), l), _resident((1, GATE_LANES), l)],
        out_specs=[pl.BlockSpec((chunks, gw, CHUNK), lambda b, j: (tile_row(b, j), 0, 0))]
                  + _shared_specs(tile_row),
        out_shape=[jax.ShapeDtypeStruct((tokens // CHUNK, gw, CHUNK), F32)] + shared_shapes,
        scratch_shapes=_state_scratch(),
        compiler_params=pltpu.CompilerParams(dimension_semantics=("arbitrary", "arbitrary"),
                                             vmem_limit_bytes=VMEM_LIMIT_BYTES),
        name=f"mix_bwd_{l}",
    )(z, z, z, *mw)


def _mix_fwd(z, h, hbt, shared, l, batch, sgun, sguw, sgub, mhn, skip, wout):
    tokens, d = h.shape
    n_tiles = tokens // batch // MIX_TILE
    tile_row = _tile_row(n_tiles, False)
    gw = GROUP_WIDTH
    chunks = MIX_TILE // CHUNK
    group = lambda g: (lambda b, j: (tile_row(b, j), g))
    return pl.pallas_call(
        _mix_fwd_kernel,
        grid=(batch, n_tiles),
        in_specs=[pl.BlockSpec((MIX_TILE, gw), group(0)),
                  pl.BlockSpec((MIX_TILE, gw), group(1)),
                  pl.BlockSpec((MIX_TILE, gw), group(3)),
                  pl.BlockSpec((MIX_TILE, d), group(0)),
                  pl.BlockSpec((chunks, gw, CHUNK), lambda b, j: (tile_row(b, j), 0, 0))]
                 + _shared_specs(tile_row)
                 + [_resident((HEADS, HEAD_DIM), l), _resident((HEADS, CHUNK, CHUNK), l),
                    _resident((HEADS, CHUNK, HEAD_DIM), l), _resident((HEADS, HEAD_DIM, CHUNK), l),
                    _resident((1, gw), l), _resident((2 * gw, d), l)],
        out_specs=pl.BlockSpec((MIX_TILE, d), group(0)),
        out_shape=jax.ShapeDtypeStruct((tokens, d), F32),
        scratch_shapes=_state_scratch() + [pltpu.VMEM((chunks, gw, CHUNK), F32),
                                           pltpu.VMEM((MIX_TILE, 2 * gw), BF16)],
        compiler_params=pltpu.CompilerParams(dimension_semantics=("arbitrary", "arbitrary"),
                                             vmem_limit_bytes=VMEM_LIMIT_BYTES),
        name=f"mix_fwd_{l}",
    )(z, z, z, h, hbt, *shared, sgun, sguw, sgub, mhn, skip, wout)


def _diag_blocks(w, transpose=False):
    depth, groups, bi, bo = w.shape
    per = MXU_WIDTH // bi
    wb = w.reshape(depth, groups // per, per, bi, bo)
    eye = jnp.eye(per, dtype=w.dtype)
    dense = wb[:, :, :, :, None, :] * eye[None, None, :, None, :, None]
    dense = dense.reshape(depth, groups // per, MXU_WIDTH, MXU_WIDTH)
    if transpose:
        dense = jnp.swapaxes(dense, -1, -2)
    return dense.astype(BF16)


def _pack_gates(w_fwd, b_fwd, w_bwd, b_bwd):
    depth, rows, n = w_fwd.shape
    w = jnp.concatenate([w_fwd, w_bwd, jnp.zeros((depth, rows, GATE_LANES - 2 * n), F32)], axis=-1)
    b = jnp.concatenate([b_fwd, b_bwd, jnp.zeros((depth, GATE_LANES - 2 * n), F32)], axis=-1)
    return w.astype(BF16), b.reshape(depth, 1, GATE_LANES)


def kernel(x, ffn1_norm, ffn1_w_gate, ffn1_w_up, ffn1_w_down, mix_norm, w_in, sgu_norm, sgu_w, sgu_b, conv_w, conv_b, w_q, w_k, w_v, gate_w_fwd, gate_b_fwd, gate_w_bwd, gate_b_bwd, mh_norm, mlstm_skip, w_out, ffn2_norm, ffn2_w_gate, ffn2_w_up, ffn2_w_down, final_norm):
    batch, seq, d = x.shape
    depth = w_in.shape[0]
    assert seq % MIX_TILE == 0 and (batch * seq) % FFN_TILE == 0
    assert sgu_w.shape[1:] == (HEADS, CHUNK, CHUNK) and mh_norm.shape[1:] == (HEADS, HEAD_DIM)
    assert gate_w_fwd.shape[1:] == (3 * GROUP_WIDTH, 2 * HEADS)

    bf = lambda w: w.astype(BF16)
    row3 = lambda g: g.reshape(depth, 1, -1)
    f1 = (row3(ffn1_norm), bf(ffn1_w_gate), bf(ffn1_w_up), bf(ffn1_w_down))
    f2 = (row3(ffn2_norm), bf(ffn2_w_gate), bf(ffn2_w_up), bf(ffn2_w_down))
    wgate, bgate = _pack_gates(gate_w_fwd, gate_b_fwd, gate_w_bwd, gate_b_bwd)
    mw = (conv_w, row3(conv_b), _diag_blocks(w_q), _diag_blocks(w_k), _diag_blocks(w_v),
          _diag_blocks(w_v, transpose=True), wgate, bgate)
    sgub = jnp.broadcast_to(sgu_b[..., None], sgu_b.shape + (HEAD_DIM,))
    mhn = jnp.broadcast_to(mh_norm[..., None], mh_norm.shape + (CHUNK,))
    gfin = final_norm.reshape(1, d)

    xt = x.reshape(batch * seq, d)
    for l in range(depth):
        h, z = _ffn_in(xt, l, *f1, row3(mix_norm), bf(w_in))
        hbt, *shared = _mix_bwd(z, l, batch, mw)
        h2 = _mix_fwd(z, h, hbt, shared, l, batch, sgu_norm, bf(sgu_w), sgub, mhn,
                      row3(mlstm_skip), bf(w_out))
        xt = _ffn_out(h2, l, *f2, gfin, final_norm=(l == depth - 1))
    return xt.reshape(batch, seq, d)
```

```python
import functools

import jax
import jax.numpy as jnp
from jax import lax
from jax.experimental import pallas as pl
from jax.experimental.pallas import tpu as pltpu

F32 = jnp.float32
BF16 = jnp.bfloat16

EPS = 1e-6
HEADS = 4
HEAD_DIM = 128
CHUNK = 128
GROUP_WIDTH = HEADS * HEAD_DIM
CONV_WIDTH = 5
QKV_BLOCK = 4
MXU_WIDTH = 256
GATE_LANES = 128
IG_LANE = {False: 0, True: 2 * HEADS}
B_LANE = {False: HEADS, True: 3 * HEADS}
STATE_ROWS = HEAD_DIM + 16
HALO_ROWS = 8

FFN_TILE = 512
MIX_TILE = 512
FFN_HIDDEN_CHUNK = 1024
VMEM_LIMIT_BYTES = 56 * 1024 * 1024

_NT = (((1,), (1,)), ((), ()))


def _rms_norm(x, g):
    return x * lax.rsqrt(jnp.mean(x * x, axis=-1, keepdims=True) + EPS) * g


def _layer_norm(x, g):
    mu = jnp.mean(x, axis=-1, keepdims=True)
    xc = x - mu
    var = jnp.mean(xc * xc, axis=-1, keepdims=True)
    return xc * lax.rsqrt(var + EPS) * g


def _log_sigmoid(x):
    return jnp.minimum(x, 0.0) - jnp.log1p(jnp.exp(-jnp.abs(x)))


def _swiglu(xn, wg_ref, wu_ref, wd_ref):
    hidden = wg_ref.shape[1]
    acc = None
    for lo in range(0, hidden, FFN_HIDDEN_CHUNK):
        hi = min(lo + FFN_HIDDEN_CHUNK, hidden)
        gate = jnp.dot(xn, wg_ref[:, lo:hi], preferred_element_type=F32)
        up = jnp.dot(xn, wu_ref[:, lo:hi], preferred_element_type=F32)
        act = (jax.nn.silu(gate) * up).astype(BF16)
        part = jnp.dot(act, wd_ref[lo:hi, :], preferred_element_type=F32)
        acc = part if acc is None else acc + part
    return acc


def _ffn_in_kernel(x_ref, g1_ref, wg_ref, wu_ref, wd_ref, g2_ref, win_ref, h_ref, z_ref):
    x = x_ref[...]
    xn = _rms_norm(x, g1_ref[...]).astype(BF16)
    h = x + 0.5 * _swiglu(xn, wg_ref, wu_ref, wd_ref)
    h_ref[...] = h
    hn = _rms_norm(h, g2_ref[...]).astype(BF16)
    z_ref[...] = jnp.dot(hn, win_ref[...], preferred_element_type=F32)


def _ffn_out_kernel(final_norm, h_ref, g1_ref, wg_ref, wu_ref, wd_ref, gf_ref, o_ref):
    h = h_ref[...]
    hn = _rms_norm(h, g1_ref[...]).astype(BF16)
    x = h + 0.5 * _swiglu(hn, wg_ref, wu_ref, wd_ref)
    if final_norm:
        x = _rms_norm(x, gf_ref[...])
    o_ref[...] = x


def _block_diag_dot(x, bd_ref):
    parts = [jnp.dot(x[:, i * MXU_WIDTH:(i + 1) * MXU_WIDTH], bd_ref[i], preferred_element_type=F32)
             for i in range(bd_ref.shape[0])]
    return jnp.concatenate(parts, axis=1)


def _mlstm_prepare(first, last, xm_ref, xprev_ref, xnext_ref, convw_ref, convb_ref,
                   bdq_ref, bdk_ref, bdv_ref, bdqt_ref, bdvt_ref, wgate_ref, bgate_ref,
                   xc_ref, qt_ref, k_ref, vt_ref, ga_ref, gr_ref):
    tile = xm_ref.shape[0]
    xm = xm_ref[...]
    prev = jnp.where(first, 0.0, xprev_ref[...])
    nxt = jnp.where(last, 0.0, xnext_ref[...])
    ext = jnp.concatenate([prev, xm, nxt], axis=0)
    n_ext = tile + 2 * HALO_ROWS
    conv = None
    for tap in range(CONV_WIDTH):
        off = tap - CONV_WIDTH // 2
        shifted = ext if off == 0 else pltpu.roll(ext, (n_ext - off) % n_ext, axis=0)
        term = shifted[HALO_ROWS:HALO_ROWS + tile] * convw_ref[tap:tap + 1, :]
        conv = term if conv is None else conv + term
    xc = jax.nn.silu(conv + convb_ref[...])
    xc_ref[...] = xc

    xc_b = xc.astype(BF16)
    xm_b = xm.astype(BF16)
    q = _block_diag_dot(xc_b, bdq_ref).astype(BF16)
    k = (_block_diag_dot(xc_b, bdk_ref) * (HEAD_DIM ** -0.5)).astype(BF16)
    v = _block_diag_dot(xm_b, bdv_ref).astype(BF16)
    k_ref[...] = k
    for i in range(bdvt_ref.shape[0]):
        blk = slice(i * MXU_WIDTH, (i + 1) * MXU_WIDTH)
        qt = lax.dot_general(bdqt_ref[i], xc_b[:, blk], _NT, preferred_element_type=F32).astype(BF16)
        vt = lax.dot_general(bdvt_ref[i], xm_b[:, blk], _NT, preferred_element_type=F32)
        for c in range(tile // CHUNK):
            qt_ref[c, blk, :] = qt[:, c * CHUNK:(c + 1) * CHUNK]
            vt_ref[c, blk, :] = vt[:, c * CHUNK:(c + 1) * CHUNK]

    gates = (jnp.dot(q, wgate_ref[0:GROUP_WIDTH, :], preferred_element_type=F32)
             + jnp.dot(k, wgate_ref[GROUP_WIDTH:2 * GROUP_WIDTH, :], preferred_element_type=F32)
             + jnp.dot(v, wgate_ref[2 * GROUP_WIDTH:, :], preferred_element_type=F32)
             + bgate_ref[...])
    logf = _log_sigmoid(gates)
    lf_hi = logf.astype(BF16)
    rem = logf - lf_hi.astype(F32)
    lf_mid = rem.astype(BF16)
    lf_lo = (rem - lf_mid.astype(F32)).astype(BF16)

    row = lax.broadcasted_iota(jnp.int32, (CHUNK, CHUNK), 0)
    col = lax.broadcasted_iota(jnp.int32, (CHUNK, CHUNK), 1)
    prefix_fwd = (col <= row).astype(BF16)
    prefix_bwd = (col >= row).astype(BF16)
    lane = lax.broadcasted_iota(jnp.int32, (CHUNK, GATE_LANES), 1)
    is_b_fwd = (lane >= B_LANE[False]) & (lane < B_LANE[False] + HEADS)
    is_b_bwd = (lane >= B_LANE[True]) & (lane < B_LANE[True] + HEADS)

    def prefix_sum(prefix, rows):
        return (jnp.dot(prefix, lf_hi[rows], preferred_element_type=F32)
                + jnp.dot(prefix, lf_mid[rows], preferred_element_type=F32)
                + jnp.dot(prefix, lf_lo[rows], preferred_element_type=F32))

    for c in range(tile // CHUNK):
        rows = slice(c * CHUNK, (c + 1) * CHUNK)
        gc = jnp.where(is_b_fwd, prefix_sum(prefix_fwd, rows),
                       jnp.where(is_b_bwd, prefix_sum(prefix_bwd, rows), gates[rows]))
        gr_ref[rows, :] = gc.T
        ga_ref[rows, :] = gc - pltpu.roll(gc, GATE_LANES - HEADS, axis=1)


def _mlstm_direction(reverse, n_chunks, qt_ref, k_ref, vt_ref, ga_ref, gr_ref, state_s, m_s, out_ref):
    key = lax.broadcasted_iota(jnp.int32, (CHUNK, CHUNK), 0)
    qry = lax.broadcasted_iota(jnp.int32, (CHUNK, CHUNK), 1)
    visible = (key >= qry) if reverse else (key <= qry)
    first_row = lax.broadcasted_iota(jnp.int32, (STATE_ROWS - HEAD_DIM, CHUNK), 0) == 0
    edge = 0 if reverse else CHUNK - 1
    ig0, b0 = IG_LANE[reverse], B_LANE[reverse]

    order = [(n_chunks - 1 - i) if reverse else i for i in range(n_chunks)]
    head_cols = [slice(h * HEAD_DIM, (h + 1) * HEAD_DIM) for h in range(HEADS)]
    chunk_rows = [slice(c * CHUNK, (c + 1) * CHUNK) for c in range(n_chunks)]
    gr = [gr_ref[chunk_rows[c], :] for c in range(n_chunks)]
    b_rows = {(c, h): gr[c][b0 + h:b0 + h + 1, :] for c in order for h in range(HEADS)}

    incr = {}
    for c in order:
        for h in range(HEADS):
            b_row = b_rows[c, h]
            ig_row = gr[c][ig0 + h:ig0 + h + 1, :]
            b_edge = jnp.broadcast_to(b_row[:, edge:edge + 1], b_row.shape)
            dec = (b_edge - b_row) + ig_row
            dec_max = jnp.broadcast_to(jnp.max(dec, axis=1, keepdims=True), dec.shape)
            wk = jnp.exp(dec - dec_max)
            lhs = jnp.concatenate([vt_ref[c, head_cols[h], :] * wk, jnp.where(first_row, wk, 0.0)],
                                  axis=0).astype(BF16)
            update = jnp.dot(lhs, k_ref[chunk_rows[c], head_cols[h]], preferred_element_type=F32)
            incr[c, h] = (b_edge, dec_max, update)

    states = [state_s[h] for h in range(HEADS)]
    ms = [m_s[h:h + 1, :] for h in range(HEADS)]
    partial = {}
    for c in order:
        ga = ga_ref[chunk_rows[c], :]
        for h in range(HEADS):
            b_row = b_rows[c, h]
            a_col = ga[:, ig0 + h:ig0 + h + 1]
            state, m_prev = states[h], ms[h]
            d = jnp.where(visible, b_row + a_col, -jnp.inf)
            m_loc = jnp.max(d, axis=0, keepdims=True)
            lhs = jnp.concatenate([k_ref[chunk_rows[c], head_cols[h]], state.astype(BF16)], axis=0)
            both = jnp.dot(lhs, qt_ref[c, head_cols[h], :], preferred_element_type=F32)
            st = both[:CHUNK] * jnp.exp(d - m_loc)
            carried = both[CHUNK:]
            inter = b_row + m_prev
            m_j = jnp.maximum(inter, m_loc)
            w_inter = jnp.exp(inter - m_j)
            w_loc = jnp.exp(m_loc - m_j)
            nq = (w_inter * carried[HEAD_DIM:HEAD_DIM + 1]
                  + w_loc * jnp.sum(st, axis=0, keepdims=True))
            inv = 1.0 / jnp.maximum(jnp.abs(nq), jnp.exp(-m_j))
            partial[c, h] = (st.astype(BF16), w_inter * inv, w_loc * inv, carried[:HEAD_DIM])

            b_edge, dec_max, update = incr[c, h]
            m_new = jnp.maximum(b_edge + m_prev, dec_max)
            states[h] = (jnp.exp(b_edge + m_prev - m_new) * state
                         + jnp.exp(dec_max - m_new) * update)
            ms[h] = m_new
    for h in range(HEADS):
        state_s[h] = states[h]
        m_s[h:h + 1, :] = ms[h]

    for c in order:
        for h in range(HEADS):
            st_b, s_inter, s_loc, carried = partial[c, h]
            num_loc = jnp.dot(vt_ref[c, head_cols[h], :].astype(BF16), st_b, preferred_element_type=F32)
            out_ref[c, head_cols[h], :] = s_inter * carried + s_loc * num_loc


def _reset_state(state_s, m_s):
    state_s[...] = jnp.zeros_like(state_s)
    m_s[...] = jnp.zeros_like(m_s)


def _mix_bwd_kernel(n_tiles, xm_ref, xprev_ref, xnext_ref, convw_ref, convb_ref, bdq_ref, bdk_ref,
                    bdv_ref, bdqt_ref, bdvt_ref, wgate_ref, bgate_ref,
                    hbt_ref, xc_ref, qt_ref, k_ref, vt_ref, ga_ref, gr_ref, state_s, m_s):
    j = pl.program_id(1)
    seq_tile = n_tiles - 1 - j

    @pl.when(j == 0)
    def _():
        _reset_state(state_s, m_s)

    _mlstm_prepare(seq_tile == 0, seq_tile == n_tiles - 1, xm_ref, xprev_ref, xnext_ref,
                   convw_ref, convb_ref, bdq_ref, bdk_ref, bdv_ref, bdqt_ref, bdvt_ref, wgate_ref,
                   bgate_ref, xc_ref, qt_ref, k_ref, vt_ref, ga_ref, gr_ref)
    _mlstm_direction(True, xm_ref.shape[0] // CHUNK, qt_ref, k_ref, vt_ref, ga_ref, gr_ref,
                     state_s, m_s, hbt_ref)


def _mix_fwd_kernel(u_ref, v_ref, og_ref, h_ref, hbt_ref, xc_ref, qt_ref, k_ref, vt_ref, ga_ref, gr_ref,
                    sgun_ref, sguw_ref, sgub_ref, mhn_ref, skip_ref, wout_ref, o_ref,
                    state_s, m_s, hft_s, y_s):
    tile = h_ref.shape[0]

    @pl.when(pl.program_id(1) == 0)
    def _():
        _reset_state(state_s, m_s)

    _mlstm_direction(False, tile // CHUNK, qt_ref, k_ref, vt_ref, ga_ref, gr_ref, state_s, m_s, hft_s)

    for h in range(HEADS):
        cols = slice(h * HEAD_DIM, (h + 1) * HEAD_DIM)
        vn = _layer_norm(jax.nn.gelu(v_ref[:, cols]), sgun_ref[h:h + 1, :]).astype(BF16)
        ug = jax.nn.gelu(u_ref[:, cols])
        for c in range(tile // CHUNK):
            rows = slice(c * CHUNK, (c + 1) * CHUNK)
            mixed = jnp.dot(sguw_ref[h], vn[rows], preferred_element_type=F32) + sgub_ref[h]
            y_s[rows, cols] = (ug[rows] * mixed).astype(BF16)

    for c in range(tile // CHUNK):
        rows = slice(c * CHUNK, (c + 1) * CHUNK)
        for h in range(HEADS):
            cols = slice(h * HEAD_DIM, (h + 1) * HEAD_DIM)
            ht = hft_s[c, cols, :] + hbt_ref[c, cols, :]
            mu = jnp.mean(ht, axis=0, keepdims=True)
            hc = ht - mu
            var = jnp.mean(hc * hc, axis=0, keepdims=True)
            hn = (hc * lax.rsqrt(var + EPS) * mhn_ref[h]).T
            ym = (hn + skip_ref[:, cols] * xc_ref[rows, cols]) * jax.nn.sigmoid(og_ref[rows, cols])
            y_s[rows, GROUP_WIDTH + h * HEAD_DIM:GROUP_WIDTH + (h + 1) * HEAD_DIM] = ym.astype(BF16)

    o_ref[...] = h_ref[...] + jnp.dot(y_s[...], wout_ref[...], preferred_element_type=F32)


def _resident(shape, layer=None):
    nd = len(shape)
    if layer is None:
        return pl.BlockSpec(shape, lambda *_: (0,) * nd, pipeline_mode=pl.Buffered(1))
    return pl.BlockSpec((None,) + shape, lambda *_: (layer,) + (0,) * nd,
                        pipeline_mode=pl.Buffered(1))


def _ffn_in(x, l, g1, wg, wu, wd, g2, win):
    tokens, d = x.shape
    hidden = wg.shape[-1]
    zw = win.shape[-1]
    row = lambda i: (i, 0)
    return pl.pallas_call(
        _ffn_in_kernel,
        grid=(tokens // FFN_TILE,),
        in_specs=[pl.BlockSpec((FFN_TILE, d), row),
                  _resident((1, d), l), _resident((d, hidden), l), _resident((d, hidden), l),
                  _resident((hidden, d), l), _resident((1, d), l), _resident((d, zw), l)],
        out_specs=[pl.BlockSpec((FFN_TILE, d), row), pl.BlockSpec((FFN_TILE, zw), row)],
        out_shape=[jax.ShapeDtypeStruct((tokens, d), F32), jax.ShapeDtypeStruct((tokens, zw), F32)],
        compiler_params=pltpu.CompilerParams(dimension_semantics=("arbitrary",),
                                             vmem_limit_bytes=VMEM_LIMIT_BYTES),
        name=f"ffn_in_{l}",
    )(x, g1, wg, wu, wd, g2, win)


def _ffn_out(h, l, g1, wg, wu, wd, gf, final_norm):
    tokens, d = h.shape
    hidden = wg.shape[-1]
    row = lambda i: (i, 0)
    return pl.pallas_call(
        functools.partial(_ffn_out_kernel, final_norm),
        grid=(tokens // FFN_TILE,),
        in_specs=[pl.BlockSpec((FFN_TILE, d), row),
                  _resident((1, d), l), _resident((d, hidden), l), _resident((d, hidden), l),
                  _resident((hidden, d), l), _resident((1, d))],
        out_specs=pl.BlockSpec((FFN_TILE, d), row),
        out_shape=jax.ShapeDtypeStruct((tokens, d), F32),
        compiler_params=pltpu.CompilerParams(dimension_semantics=("arbitrary",),
                                             vmem_limit_bytes=VMEM_LIMIT_BYTES),
        name=f"ffn_out_{l}",
    )(h, g1, wg, wu, wd, gf)


def _state_scratch():
    return [pltpu.VMEM((HEADS, STATE_ROWS, HEAD_DIM), F32),
            pltpu.VMEM((8, 128), F32)]


def _tile_row(n_tiles, reverse):
    return lambda b, j: b * n_tiles + ((n_tiles - 1 - j) if reverse else j)


def _shared_specs(tile_row):
    gw = GROUP_WIDTH
    chunks = MIX_TILE // CHUNK
    rows2 = lambda b, j: (tile_row(b, j), 0)
    rows3 = lambda b, j: (tile_row(b, j), 0, 0)
    return [pl.BlockSpec((MIX_TILE, gw), rows2),
            pl.BlockSpec((chunks, gw, CHUNK), rows3),
            pl.BlockSpec((MIX_TILE, gw), rows2),
            pl.BlockSpec((chunks, gw, CHUNK), rows3),
            pl.BlockSpec((MIX_TILE, GATE_LANES), rows2),
            pl.BlockSpec((MIX_TILE, CHUNK), rows2)]


def _mix_bwd(z, l, batch, mw):
    tokens = z.shape[0]
    n_tiles = tokens // batch // MIX_TILE
    tile_row = _tile_row(n_tiles, True)
    gw = GROUP_WIDTH
    halo_per_tile = MIX_TILE // HALO_ROWS
    last_halo = tokens // HALO_ROWS - 1
    xm_group = 2
    n_diag = gw // MXU_WIDTH
    chunks = MIX_TILE // CHUNK
    shared_shapes = [jax.ShapeDtypeStruct((tokens, gw), F32),
                     jax.ShapeDtypeStruct((tokens // CHUNK, gw, CHUNK), BF16),
                     jax.ShapeDtypeStruct((tokens, gw), BF16),
                     jax.ShapeDtypeStruct((tokens // CHUNK, gw, CHUNK), F32),
                     jax.ShapeDtypeStruct((tokens, GATE_LANES), F32),
                     jax.ShapeDtypeStruct((tokens, CHUNK), F32)]
    return pl.pallas_call(
        functools.partial(_mix_bwd_kernel, n_tiles),
        grid=(batch, n_tiles),
        in_specs=[pl.BlockSpec((MIX_TILE, gw), lambda b, j: (tile_row(b, j), xm_group)),
                  pl.BlockSpec((HALO_ROWS, gw), lambda b, j: (
                      jnp.maximum(tile_row(b, j) * halo_per_tile - 1, 0), xm_group)),
                  pl.BlockSpec((HALO_ROWS, gw), lambda b, j: (
                      jnp.minimum((tile_row(b, j) + 1) * halo_per_tile, last_halo), xm_group)),
                  _resident((CONV_WIDTH, gw), l), _resident((1, gw), l),
                  _resident((n_diag, MXU_WIDTH, MXU_WIDTH), l),
                  _resident((n_diag, MXU_WIDTH, MXU_WIDTH), l),
                  _resident((n_diag, MXU_WIDTH, MXU_WIDTH), l),
                  _resident((n_diag, MXU_WIDTH, MXU_WIDTH), l),
                  _resident((n_diag, MXU_WIDTH, MXU_WIDTH), l),
                  _resident((3 * gw, GATE_LANES), l), _resident((1, GATE_LANES), l)],
        out_specs=[pl.BlockSpec((chunks, gw, CHUNK), lambda b, j: (tile_row(b, j), 0, 0))]
                  + _shared_specs(tile_row),
        out_shape=[jax.ShapeDtypeStruct((tokens // CHUNK, gw, CHUNK), F32)] + shared_shapes,
        scratch_shapes=_state_scratch(),
        compiler_params=pltpu.CompilerParams(dimension_semantics=("arbitrary", "arbitrary"),
                                             vmem_limit_bytes=VMEM_LIMIT_BYTES),
        name=f"mix_bwd_{l}",
    )(z, z, z, *mw)


def _mix_fwd(z, h, hbt, shared, l, batch, sgun, sguw, sgub, mhn, skip, wout):
    tokens, d = h.shape
    n_tiles = tokens // batch // MIX_TILE
    tile_row = _tile_row(n_tiles, False)
    gw = GROUP_WIDTH
    chunks = MIX_TILE // CHUNK
    group = lambda g: (lambda b, j: (tile_row(b, j), g))
    return pl.pallas_call(
        _mix_fwd_kernel,
        grid=(batch, n_tiles),
        in_specs=[pl.BlockSpec((MIX_TILE, gw), group(0)),
                  pl.BlockSpec((MIX_TILE, gw), group(1)),
                  pl.BlockSpec((MIX_TILE, gw), group(3)),
                  pl.BlockSpec((MIX_TILE, d), group(0)),
                  pl.BlockSpec((chunks, gw, CHUNK), lambda b, j: (tile_row(b, j), 0, 0))]
                 + _shared_specs(tile_row)
                 + [_resident((HEADS, HEAD_DIM), l), _resident((HEADS, CHUNK, CHUNK), l),
                    _resident((HEADS, CHUNK, HEAD_DIM), l), _resident((HEADS, HEAD_DIM, CHUNK), l),
                    _resident((1, gw), l), _resident((2 * gw, d), l)],
        out_specs=pl.BlockSpec((MIX_TILE, d), group(0)),
        out_shape=jax.ShapeDtypeStruct((tokens, d), F32),
        scratch_shapes=_state_scratch() + [pltpu.VMEM((chunks, gw, CHUNK), F32),
                                           pltpu.VMEM((MIX_TILE, 2 * gw), BF16)],
        compiler_params=pltpu.CompilerParams(dimension_semantics=("arbitrary", "arbitrary"),
                                             vmem_limit_bytes=VMEM_LIMIT_BYTES),
        name=f"mix_fwd_{l}",
    )(z, z, z, h, hbt, *shared, sgun, sguw, sgub, mhn, skip, wout)


def _diag_blocks(w, transpose=False):
    depth, groups, bi, bo = w.shape
    per = MXU_WIDTH // bi
    wb = w.reshape(depth, groups // per, per, bi, bo)
    eye = jnp.eye(per, dtype=w.dtype)
    dense = wb[:, :, :, :, None, :] * eye[None, None, :, None, :, None]
    dense = dense.reshape(depth, groups // per, MXU_WIDTH, MXU_WIDTH)
    if transpose:
        dense = jnp.swapaxes(dense, -1, -2)
    return dense.astype(BF16)


def _pack_gates(w_fwd, b_fwd, w_bwd, b_bwd):
    depth, rows, n = w_fwd.shape
    w = jnp.concatenate([w_fwd, w_bwd, jnp.zeros((depth, rows, GATE_LANES - 2 * n), F32)], axis=-1)
    b = jnp.concatenate([b_fwd, b_bwd, jnp.zeros((depth, GATE_LANES - 2 * n), F32)], axis=-1)
    return w.astype(BF16), b.reshape(depth, 1, GATE_LANES)


def kernel(x, ffn1_norm, ffn1_w_gate, ffn1_w_up, ffn1_w_down, mix_norm, w_in, sgu_norm, sgu_w, sgu_b, conv_w, conv_b, w_q, w_k, w_v, gate_w_fwd, gate_b_fwd, gate_w_bwd, gate_b_bwd, mh_norm, mlstm_skip, w_out, ffn2_norm, ffn2_w_gate, ffn2_w_up, ffn2_w_down, final_norm):
    batch, seq, d = x.shape
    depth = w_in.shape[0]
    assert seq % MIX_TILE == 0 and (batch * seq) % FFN_TILE == 0
    assert sgu_w.shape[1:] == (HEADS, CHUNK, CHUNK) and mh_norm.shape[1:] == (HEADS, HEAD_DIM)
    assert gate_w_fwd.shape[1:] == (3 * GROUP_WIDTH, 2 * HEADS)

    bf = lambda w: w.astype(BF16)
    row3 = lambda g: g.reshape(depth, 1, -1)
    f1 = (row3(ffn1_norm), bf(ffn1_w_gate), bf(ffn1_w_up), bf(ffn1_w_down))
    f2 = (row3(ffn2_norm), bf(ffn2_w_gate), bf(ffn2_w_up), bf(ffn2_w_down))
    wgate, bgate = _pack_gates(gate_w_fwd, gate_b_fwd, gate_w_bwd, gate_b_bwd)
    mw = (conv_w, row3(conv_b), _diag_blocks(w_q), _diag_blocks(w_k), _diag_blocks(w_v),
          _diag_blocks(w_q, transpose=True), _diag_blocks(w_v, transpose=True), wgate, bgate)
    sgub = jnp.broadcast_to(sgu_b[..., None], sgu_b.shape + (HEAD_DIM,))
    mhn = jnp.broadcast_to(mh_norm[..., None], mh_norm.shape + (CHUNK,))
    gfin = final_norm.reshape(1, d)

    xt = x.reshape(batch * seq, d)
    for l in range(depth):
        h, z = _ffn_in(xt, l, *f1, row3(mix_norm), bf(w_in))
        hbt, *shared = _mix_bwd(z, l, batch, mw)
        h2 = _mix_fwd(z, h, hbt, shared, l, batch, sgu_norm, bf(sgu_w), sgub, mhn,
                      row3(mlstm_skip), bf(w_out))
        xt = _ffn_out(h2, l, *f2, gfin, final_norm=(l == depth - 1))
    return xt.reshape(batch, seq, d)
```

```python
import functools

import jax
import jax.numpy as jnp
from jax import lax
from jax.experimental import pallas as pl
from jax.experimental.pallas import tpu as pltpu

F32 = jnp.float32
BF16 = jnp.bfloat16

EPS = 1e-6
HEADS = 4
HEAD_DIM = 128
CHUNK = 128
GROUP_WIDTH = HEADS * HEAD_DIM
CONV_WIDTH = 5
QKV_BLOCK = 4
MXU_WIDTH = 256
GATE_LANES = 128
IG_LANE = {False: 0, True: 2 * HEADS}
B_LANE = {False: HEADS, True: 3 * HEADS}
STATE_ROWS = HEAD_DIM + 16
HALO_ROWS = 8

FFN_TILE = 512
MIX_TILE = 512
FFN_HIDDEN_CHUNK = 1024
VMEM_LIMIT_BYTES = 56 * 1024 * 1024

_NT = (((1,), (1,)), ((), ()))


def _rms_norm(x, g):
    return x * lax.rsqrt(jnp.mean(x * x, axis=-1, keepdims=True) + EPS) * g


def _layer_norm(x, g):
    mu = jnp.mean(x, axis=-1, keepdims=True)
    xc = x - mu
    var = jnp.mean(xc * xc, axis=-1, keepdims=True)
    return xc * lax.rsqrt(var + EPS) * g


def _log_sigmoid(x):
    return jnp.minimum(x, 0.0) - jnp.log1p(jnp.exp(-jnp.abs(x)))


def _swiglu(xn, wg_ref, wu_ref, wd_ref):
    hidden = wg_ref.shape[1]
    acc = None
    for lo in range(0, hidden, FFN_HIDDEN_CHUNK):
        hi = min(lo + FFN_HIDDEN_CHUNK, hidden)
        gate = jnp.dot(xn, wg_ref[:, lo:hi], preferred_element_type=F32)
        up = jnp.dot(xn, wu_ref[:, lo:hi], preferred_element_type=F32)
        act = (jax.nn.silu(gate) * up).astype(BF16)
        part = jnp.dot(act, wd_ref[lo:hi, :], preferred_element_type=F32)
        acc = part if acc is None else acc + part
    return acc


def _short_conv(rows, convw_ref, convb_ref):
    n = rows.shape[0]
    conv = None
    for tap in range(CONV_WIDTH):
        off = tap - CONV_WIDTH // 2
        shifted = rows if off == 0 else pltpu.roll(rows, (n - off) % n, axis=0)
        term = shifted * convw_ref[tap:tap + 1, :]
        conv = term if conv is None else conv + term
    return jax.nn.silu(conv + convb_ref[...])


def _ffn_in_kernel(x_ref, g1_ref, wg_ref, wu_ref, wd_ref, g2_ref, win_ref, sgun_ref, convw_ref, convb_ref,
                   h_ref, ug_ref, vn_ref, xm_ref, xcr_ref, sg_ref):
    x = x_ref[...]
    xn = _rms_norm(x, g1_ref[...]).astype(BF16)
    h = x + 0.5 * _swiglu(xn, wg_ref, wu_ref, wd_ref)
    h_ref[...] = h
    hn = _rms_norm(h, g2_ref[...]).astype(BF16)
    z = jnp.dot(hn, win_ref[...], preferred_element_type=F32)
    gw = GROUP_WIDTH
    ug_ref[...] = jax.nn.gelu(z[:, :gw])
    for hd in range(HEADS):
        cols = slice(hd * HEAD_DIM, (hd + 1) * HEAD_DIM)
        v = z[:, gw + hd * HEAD_DIM:gw + (hd + 1) * HEAD_DIM]
        vn_ref[:, cols] = _layer_norm(jax.nn.gelu(v), sgun_ref[hd:hd + 1, :]).astype(BF16)
    xm = z[:, 2 * gw:3 * gw]
    xm_ref[...] = xm
    xcr_ref[...] = _short_conv(xm, convw_ref, convb_ref)
    sg_ref[...] = jax.nn.sigmoid(z[:, 3 * gw:])


def _ffn_out_kernel(final_norm, h_ref, g1_ref, wg_ref, wu_ref, wd_ref, gf_ref, o_ref):
    h = h_ref[...]
    hn = _rms_norm(h, g1_ref[...]).astype(BF16)
    x = h + 0.5 * _swiglu(hn, wg_ref, wu_ref, wd_ref)
    if final_norm:
        x = _rms_norm(x, gf_ref[...])
    o_ref[...] = x


def _block_diag_dot(x, bd_ref):
    parts = [jnp.dot(x[:, i * MXU_WIDTH:(i + 1) * MXU_WIDTH], bd_ref[i], preferred_element_type=F32)
             for i in range(bd_ref.shape[0])]
    return jnp.concatenate(parts, axis=1)


def _mlstm_prepare(first, last, xm_ref, xprev_ref, xnext_ref, xcr_ref, convw_ref, convb_ref,
                   bdq_ref, bdk_ref, bdv_ref, bdqt_ref, bdvt_ref, wgate_ref, bgate_ref,
                   xc_ref, qt_ref, k_ref, vt_ref, ga_ref, gr_ref):
    tile = xm_ref.shape[0]
    xm = xm_ref[...]
    prev = jnp.where(first, 0.0, xprev_ref[...])
    nxt = jnp.where(last, 0.0, xnext_ref[...])
    r = HALO_ROWS

    def edge_conv(above, rows, below):
        ext = jnp.concatenate([above, rows, below], axis=0)
        return _short_conv(ext, convw_ref, convb_ref)[r:2 * r]

    xc = jnp.concatenate([edge_conv(prev, xm[0:r], xm[r:2 * r]),
                          xcr_ref[r:tile - r, :],
                          edge_conv(xm[tile - 2 * r:tile - r], xm[tile - r:], nxt)], axis=0)
    xc_ref[...] = xc

    xc_b = xc.astype(BF16)
    xm_b = xm.astype(BF16)
    q = _block_diag_dot(xc_b, bdq_ref).astype(BF16)
    k = (_block_diag_dot(xc_b, bdk_ref) * (HEAD_DIM ** -0.5)).astype(BF16)
    v = _block_diag_dot(xm_b, bdv_ref).astype(BF16)
    k_ref[...] = k
    for i in range(bdvt_ref.shape[0]):
        blk = slice(i * MXU_WIDTH, (i + 1) * MXU_WIDTH)
        qt = lax.dot_general(bdqt_ref[i], xc_b[:, blk], _NT, preferred_element_type=F32).astype(BF16)
        vt = lax.dot_general(bdvt_ref[i], xm_b[:, blk], _NT, preferred_element_type=F32)
        for c in range(tile // CHUNK):
            qt_ref[c, blk, :] = qt[:, c * CHUNK:(c + 1) * CHUNK]
            vt_ref[c, blk, :] = vt[:, c * CHUNK:(c + 1) * CHUNK]

    gates = (jnp.dot(q, wgate_ref[0:GROUP_WIDTH, :], preferred_element_type=F32)
             + jnp.dot(k, wgate_ref[GROUP_WIDTH:2 * GROUP_WIDTH, :], preferred_element_type=F32)
             + jnp.dot(v, wgate_ref[2 * GROUP_WIDTH:, :], preferred_element_type=F32)
             + bgate_ref[...])
    logf = _log_sigmoid(gates)
    lf_hi = logf.astype(BF16)
    rem = logf - lf_hi.astype(F32)
    lf_mid = rem.astype(BF16)
    lf_lo = (rem - lf_mid.astype(F32)).astype(BF16)

    row = lax.broadcasted_iota(jnp.int32, (CHUNK, CHUNK), 0)
    col = lax.broadcasted_iota(jnp.int32, (CHUNK, CHUNK), 1)
    prefix_fwd = (col <= row).astype(BF16)
    prefix_bwd = (col >= row).astype(BF16)
    lane = lax.broadcasted_iota(jnp.int32, (CHUNK, GATE_LANES), 1)
    is_b_fwd = (lane >= B_LANE[False]) & (lane < B_LANE[False] + HEADS)
    is_b_bwd = (lane >= B_LANE[True]) & (lane < B_LANE[True] + HEADS)

    def prefix_sum(prefix, rows):
        return (jnp.dot(prefix, lf_hi[rows], preferred_element_type=F32)
                + jnp.dot(prefix, lf_mid[rows], preferred_element_type=F32)
                + jnp.dot(prefix, lf_lo[rows], preferred_element_type=F32))

    for c in range(tile // CHUNK):
        rows = slice(c * CHUNK, (c + 1) * CHUNK)
        gc = jnp.where(is_b_fwd, prefix_sum(prefix_fwd, rows),
                       jnp.where(is_b_bwd, prefix_sum(prefix_bwd, rows), gates[rows]))
        gr_ref[rows, :] = gc.T
        ga_ref[rows, :] = gc - pltpu.roll(gc, GATE_LANES - HEADS, axis=1)


def _mlstm_direction(reverse, n_chunks, qt_ref, k_ref, vt_ref, ga_ref, gr_ref, state_s, m_s, out_ref):
    key = lax.broadcasted_iota(jnp.int32, (CHUNK, CHUNK), 0)
    qry = lax.broadcasted_iota(jnp.int32, (CHUNK, CHUNK), 1)
    visible = (key >= qry) if reverse else (key <= qry)
    first_row = lax.broadcasted_iota(jnp.int32, (STATE_ROWS - HEAD_DIM, CHUNK), 0) == 0
    edge = 0 if reverse else CHUNK - 1
    ig0, b0 = IG_LANE[reverse], B_LANE[reverse]

    order = [(n_chunks - 1 - i) if reverse else i for i in range(n_chunks)]
    head_cols = [slice(h * HEAD_DIM, (h + 1) * HEAD_DIM) for h in range(HEADS)]
    chunk_rows = [slice(c * CHUNK, (c + 1) * CHUNK) for c in range(n_chunks)]
    gr = [gr_ref[chunk_rows[c], :] for c in range(n_chunks)]
    b_rows = {(c, h): gr[c][b0 + h:b0 + h + 1, :] for c in order for h in range(HEADS)}

    incr = {}
    for c in order:
        for h in range(HEADS):
            b_row = b_rows[c, h]
            ig_row = gr[c][ig0 + h:ig0 + h + 1, :]
            b_edge = jnp.broadcast_to(b_row[:, edge:edge + 1], b_row.shape)
            dec = (b_edge - b_row) + ig_row
            dec_max = jnp.broadcast_to(jnp.max(dec, axis=1, keepdims=True), dec.shape)
            wk = jnp.exp(dec - dec_max)
            lhs = jnp.concatenate([vt_ref[c, head_cols[h], :] * wk, jnp.where(first_row, wk, 0.0)],
                                  axis=0).astype(BF16)
            update = jnp.dot(lhs, k_ref[chunk_rows[c], head_cols[h]], preferred_element_type=F32)
            incr[c, h] = (b_edge, dec_max, update)

    states = [state_s[h] for h in range(HEADS)]
    ms = [m_s[h:h + 1, :] for h in range(HEADS)]
    partial = {}
    for c in order:
        ga = ga_ref[chunk_rows[c], :]
        for h in range(HEADS):
            b_row = b_rows[c, h]
            a_col = ga[:, ig0 + h:ig0 + h + 1]
            state, m_prev = states[h], ms[h]
            d = jnp.where(visible, b_row + a_col, -jnp.inf)
            m_loc = jnp.max(d, axis=0, keepdims=True)
            lhs = jnp.concatenate([k_ref[chunk_rows[c], head_cols[h]], state.astype(BF16)], axis=0)
            both = jnp.dot(lhs, qt_ref[c, head_cols[h], :], preferred_element_type=F32)
            st = both[:CHUNK] * jnp.exp(d - m_loc)
            carried = both[CHUNK:]
            inter = b_row + m_prev
            m_j = jnp.maximum(inter, m_loc)
            w_inter = jnp.exp(inter - m_j)
            w_loc = jnp.exp(m_loc - m_j)
            nq = (w_inter * carried[HEAD_DIM:HEAD_DIM + 1]
                  + w_loc * jnp.sum(st, axis=0, keepdims=True))
            inv = 1.0 / jnp.maximum(jnp.abs(nq), jnp.exp(-m_j))
            partial[c, h] = (st.astype(BF16), w_inter * inv, w_loc * inv, carried[:HEAD_DIM])

            b_edge, dec_max, update = incr[c, h]
            m_new = jnp.maximum(b_edge + m_prev, dec_max)
            states[h] = (jnp.exp(b_edge + m_prev - m_new) * state
                         + jnp.exp(dec_max - m_new) * update)
            ms[h] = m_new
    for h in range(HEADS):
        state_s[h] = states[h]
        m_s[h:h + 1, :] = ms[h]

    for c in order:
        for h in range(HEADS):
            st_b, s_inter, s_loc, carried = partial[c, h]
            num_loc = jnp.dot(vt_ref[c, head_cols[h], :].astype(BF16), st_b, preferred_element_type=F32)
            out_ref[c, head_cols[h], :] = s_inter * carried + s_loc * num_loc


def _reset_state(state_s, m_s):
    state_s[...] = jnp.zeros_like(state_s)
    m_s[...] = jnp.zeros_like(m_s)


def _mix_bwd_kernel(n_tiles, xm_ref, xprev_ref, xnext_ref, xcr_ref, convw_ref, convb_ref, bdq_ref, bdk_ref,
                    bdv_ref, bdqt_ref, bdvt_ref, wgate_ref, bgate_ref,
                    hbt_ref, xc_ref, qt_ref, k_ref, vt_ref, ga_ref, gr_ref, state_s, m_s):
    j = pl.program_id(1)
    seq_tile = n_tiles - 1 - j

    @pl.when(j == 0)
    def _():
        _reset_state(state_s, m_s)

    _mlstm_prepare(seq_tile == 0, seq_tile == n_tiles - 1, xm_ref, xprev_ref, xnext_ref, xcr_ref,
                   convw_ref, convb_ref, bdq_ref, bdk_ref, bdv_ref, bdqt_ref, bdvt_ref, wgate_ref,
                   bgate_ref, xc_ref, qt_ref, k_ref, vt_ref, ga_ref, gr_ref)
    _mlstm_direction(True, xm_ref.shape[0] // CHUNK, qt_ref, k_ref, vt_ref, ga_ref, gr_ref,
                     state_s, m_s, hbt_ref)


def _mix_fwd_kernel(ug_ref, vn_ref, sg_ref, h_ref, hbt_ref, xc_ref, qt_ref, k_ref, vt_ref, ga_ref, gr_ref,
                    sguw_ref, sgub_ref, mhn_ref, skip_ref, wout_ref, o_ref,
                    state_s, m_s, hft_s, y_s):
    tile = h_ref.shape[0]

    @pl.when(pl.program_id(1) == 0)
    def _():
        _reset_state(state_s, m_s)

    _mlstm_direction(False, tile // CHUNK, qt_ref, k_ref, vt_ref, ga_ref, gr_ref, state_s, m_s, hft_s)

    for h in range(HEADS):
        cols = slice(h * HEAD_DIM, (h + 1) * HEAD_DIM)
        for c in range(tile // CHUNK):
            rows = slice(c * CHUNK, (c + 1) * CHUNK)
            mixed = jnp.dot(sguw_ref[h], vn_ref[rows, cols], preferred_element_type=F32) + sgub_ref[h]
            y_s[rows, cols] = (ug_ref[rows, cols] * mixed).astype(BF16)

    for c in range(tile // CHUNK):
        rows = slice(c * CHUNK, (c + 1) * CHUNK)
        for h in range(HEADS):
            cols = slice(h * HEAD_DIM, (h + 1) * HEAD_DIM)
            ht = hft_s[c, cols, :] + hbt_ref[c, cols, :]
            mu = jnp.mean(ht, axis=0, keepdims=True)
            hc = ht - mu
            var = jnp.mean(hc * hc, axis=0, keepdims=True)
            hn = (hc * lax.rsqrt(var + EPS) * mhn_ref[h]).T
            ym = (hn + skip_ref[:, cols] * xc_ref[rows, cols]) * sg_ref[rows, cols]
            y_s[rows, GROUP_WIDTH + h * HEAD_DIM:GROUP_WIDTH + (h + 1) * HEAD_DIM] = ym.astype(BF16)

    o_ref[...] = h_ref[...] + jnp.dot(y_s[...], wout_ref[...], preferred_element_type=F32)


def _resident(shape, layer=None):
    nd = len(shape)
    if layer is None:
        return pl.BlockSpec(shape, lambda *_: (0,) * nd, pipeline_mode=pl.Buffered(1))
    return pl.BlockSpec((None,) + shape, lambda *_: (layer,) + (0,) * nd,
                        pipeline_mode=pl.Buffered(1))


def _ffn_in(x, l, g1, wg, wu, wd, g2, win, sgun, convw, convb):
    tokens, d = x.shape
    hidden = wg.shape[-1]
    zw = win.shape[-1]
    gw = GROUP_WIDTH
    assert zw == 4 * gw
    row = lambda i: (i, 0)
    group_spec = pl.BlockSpec((FFN_TILE, gw), row)
    group_f32 = jax.ShapeDtypeStruct((tokens, gw), F32)
    return pl.pallas_call(
        _ffn_in_kernel,
        grid=(tokens // FFN_TILE,),
        in_specs=[pl.BlockSpec((FFN_TILE, d), row),
                  _resident((1, d), l), _resident((d, hidden), l), _resident((d, hidden), l),
                  _resident((hidden, d), l), _resident((1, d), l), _resident((d, zw), l),
                  _resident((HEADS, HEAD_DIM), l), _resident((CONV_WIDTH, gw), l), _resident((1, gw), l)],
        out_specs=[pl.BlockSpec((FFN_TILE, d), row)] + [group_spec] * 5,
        out_shape=[jax.ShapeDtypeStruct((tokens, d), F32), group_f32,
                   jax.ShapeDtypeStruct((tokens, gw), BF16), group_f32, group_f32, group_f32],
        compiler_params=pltpu.CompilerParams(dimension_semantics=("arbitrary",),
                                             vmem_limit_bytes=VMEM_LIMIT_BYTES),
        name=f"ffn_in_{l}",
    )(x, g1, wg, wu, wd, g2, win, sgun, convw, convb)


def _ffn_out(h, l, g1, wg, wu, wd, gf, final_norm):
    tokens, d = h.shape
    hidden = wg.shape[-1]
    row = lambda i: (i, 0)
    return pl.pallas_call(
        functools.partial(_ffn_out_kernel, final_norm),
        grid=(tokens // FFN_TILE,),
        in_specs=[pl.BlockSpec((FFN_TILE, d), row),
                  _resident((1, d), l), _resident((d, hidden), l), _resident((d, hidden), l),
                  _resident((hidden, d), l), _resident((1, d))],
        out_specs=pl.BlockSpec((FFN_TILE, d), row),
        out_shape=jax.ShapeDtypeStruct((tokens, d), F32),
        compiler_params=pltpu.CompilerParams(dimension_semantics=("arbitrary",),
                                             vmem_limit_bytes=VMEM_LIMIT_BYTES),
        name=f"ffn_out_{l}",
    )(h, g1, wg, wu, wd, gf)


def _state_scratch():
    return [pltpu.VMEM((HEADS, STATE_ROWS, HEAD_DIM), F32),
            pltpu.VMEM((8, 128), F32)]


def _tile_row(n_tiles, reverse):
    return lambda b, j: b * n_tiles + ((n_tiles - 1 - j) if reverse else j)


def _shared_specs(tile_row):
    gw = GROUP_WIDTH
    chunks = MIX_TILE // CHUNK
    rows2 = lambda b, j: (tile_row(b, j), 0)
    rows3 = lambda b, j: (tile_row(b, j), 0, 0)
    return [pl.BlockSpec((MIX_TILE, gw), rows2),
            pl.BlockSpec((chunks, gw, CHUNK), rows3),
            pl.BlockSpec((MIX_TILE, gw), rows2),
            pl.BlockSpec((chunks, gw, CHUNK), rows3),
            pl.BlockSpec((MIX_TILE, GATE_LANES), rows2),
            pl.BlockSpec((MIX_TILE, CHUNK), rows2)]


def _mix_bwd(xm, xcr, l, batch, mw):
    tokens = xm.shape[0]
    n_tiles = tokens // batch // MIX_TILE
    tile_row = _tile_row(n_tiles, True)
    gw = GROUP_WIDTH
    halo_per_tile = MIX_TILE // HALO_ROWS
    last_halo = tokens // HALO_ROWS - 1
    n_diag = gw // MXU_WIDTH
    chunks = MIX_TILE // CHUNK
    shared_shapes = [jax.ShapeDtypeStruct((tokens, gw), F32),
                     jax.ShapeDtypeStruct((tokens // CHUNK, gw, CHUNK), BF16),
                     jax.ShapeDtypeStruct((tokens, gw), BF16),
                     jax.ShapeDtypeStruct((tokens // CHUNK, gw, CHUNK), F32),
                     jax.ShapeDtypeStruct((tokens, GATE_LANES), F32),
                     jax.ShapeDtypeStruct((tokens, CHUNK), F32)]
    return pl.pallas_call(
        functools.partial(_mix_bwd_kernel, n_tiles),
        grid=(batch, n_tiles),
        in_specs=[pl.BlockSpec((MIX_TILE, gw), lambda b, j: (tile_row(b, j), 0)),
                  pl.BlockSpec((HALO_ROWS, gw), lambda b, j: (
                      jnp.maximum(tile_row(b, j) * halo_per_tile - 1, 0), 0)),
                  pl.BlockSpec((HALO_ROWS, gw), lambda b, j: (
                      jnp.minimum((tile_row(b, j) + 1) * halo_per_tile, last_halo), 0)),
                  pl.BlockSpec((MIX_TILE, gw), lambda b, j: (tile_row(b, j), 0)),
                  _resident((CONV_WIDTH, gw), l), _resident((1, gw), l),
                  _resident((n_diag, MXU_WIDTH, MXU_WIDTH), l),
                  _resident((n_diag, MXU_WIDTH, MXU_WIDTH), l),
                  _resident((n_diag, MXU_WIDTH, MXU_WIDTH), l),
                  _resident((n_diag, MXU_WIDTH, MXU_WIDTH), l),
                  _resident((n_diag, MXU_WIDTH, MXU_WIDTH), l),
                  _resident((3 * gw, GATE_LANES), l), _resident((1, GATE_LANES), l)],
        out_specs=[pl.BlockSpec((chunks, gw, CHUNK), lambda b, j: (tile_row(b, j), 0, 0))]
                  + _shared_specs(tile_row),
        out_shape=[jax.ShapeDtypeStruct((tokens // CHUNK, gw, CHUNK), F32)] + shared_shapes,
        scratch_shapes=_state_scratch(),
        compiler_params=pltpu.CompilerParams(dimension_semantics=("arbitrary", "arbitrary"),
                                             vmem_limit_bytes=VMEM_LIMIT_BYTES),
        name=f"mix_bwd_{l}",
    )(xm, xm, xm, xcr, *mw)


def _mix_fwd(ug, vn, sg, h, hbt, shared, l, batch, sguw, sgub, mhn, skip, wout):
    tokens, d = h.shape
    n_tiles = tokens // batch // MIX_TILE
    tile_row = _tile_row(n_tiles, False)
    gw = GROUP_WIDTH
    chunks = MIX_TILE // CHUNK
    rows2 = lambda b, j: (tile_row(b, j), 0)
    return pl.pallas_call(
        _mix_fwd_kernel,
        grid=(batch, n_tiles),
        in_specs=[pl.BlockSpec((MIX_TILE, gw), rows2),
                  pl.BlockSpec((MIX_TILE, gw), rows2),
                  pl.BlockSpec((MIX_TILE, gw), rows2),
                  pl.BlockSpec((MIX_TILE, d), rows2),
                  pl.BlockSpec((chunks, gw, CHUNK), lambda b, j: (tile_row(b, j), 0, 0))]
                 + _shared_specs(tile_row)
                 + [_resident((HEADS, CHUNK, CHUNK), l),
                    _resident((HEADS, CHUNK, HEAD_DIM), l), _resident((HEADS, HEAD_DIM, CHUNK), l),
                    _resident((1, gw), l), _resident((2 * gw, d), l)],
        out_specs=pl.BlockSpec((MIX_TILE, d), rows2),
        out_shape=jax.ShapeDtypeStruct((tokens, d), F32),
        scratch_shapes=_state_scratch() + [pltpu.VMEM((chunks, gw, CHUNK), F32),
                                           pltpu.VMEM((MIX_TILE, 2 * gw), BF16)],
        compiler_params=pltpu.CompilerParams(dimension_semantics=("arbitrary", "arbitrary"),
                                             vmem_limit_bytes=VMEM_LIMIT_BYTES),
        name=f"mix_fwd_{l}",
    )(ug, vn, sg, h, hbt, *shared, sguw, sgub, mhn, skip, wout)


def _diag_blocks(w, transpose=False):
    depth, groups, bi, bo = w.shape
    rows = w.reshape(depth, groups * bi // MXU_WIDTH, MXU_WIDTH, bo)
    idx = jnp.arange(MXU_WIDTH)
    tiled = jnp.einsum('dnro,oc->dnrc', rows, (idx[None, :] % bo == jnp.arange(bo)[:, None]).astype(F32),
                       precision=lax.Precision.HIGHEST)
    dense = jnp.where(idx[:, None] // bi == idx[None, :] // bo, tiled, 0.0)
    if transpose:
        dense = jnp.swapaxes(dense, -1, -2)
    return dense.astype(BF16)


def _pack_gates(w_fwd, b_fwd, w_bwd, b_bwd):
    depth, rows, n = w_fwd.shape
    w = jnp.concatenate([w_fwd, w_bwd, jnp.zeros((depth, rows, GATE_LANES - 2 * n), F32)], axis=-1)
    b = jnp.concatenate([b_fwd, b_bwd, jnp.zeros((depth, GATE_LANES - 2 * n), F32)], axis=-1)
    return w.astype(BF16), b.reshape(depth, 1, GATE_LANES)


def kernel(x, ffn1_norm, ffn1_w_gate, ffn1_w_up, ffn1_w_down, mix_norm, w_in, sgu_norm, sgu_w, sgu_b, conv_w, conv_b, w_q, w_k, w_v, gate_w_fwd, gate_b_fwd, gate_w_bwd, gate_b_bwd, mh_norm, mlstm_skip, w_out, ffn2_norm, ffn2_w_gate, ffn2_w_up, ffn2_w_down, final_norm):
    batch, seq, d = x.shape
    depth = w_in.shape[0]
    assert seq % MIX_TILE == 0 and (batch * seq) % FFN_TILE == 0
    assert sgu_w.shape[1:] == (HEADS, CHUNK, CHUNK) and mh_norm.shape[1:] == (HEADS, HEAD_DIM)
    assert gate_w_fwd.shape[1:] == (3 * GROUP_WIDTH, 2 * HEADS)

    bf = lambda w: w.astype(BF16)
    row3 = lambda g: g.reshape(depth, 1, -1)
    f1 = (row3(ffn1_norm), bf(ffn1_w_gate), bf(ffn1_w_up), bf(ffn1_w_down))
    f2 = (row3(ffn2_norm), bf(ffn2_w_gate), bf(ffn2_w_up), bf(ffn2_w_down))
    wgate, bgate = _pack_gates(gate_w_fwd, gate_b_fwd, gate_w_bwd, gate_b_bwd)
    convb = row3(conv_b)
    mw = (conv_w, convb, _diag_blocks(w_q), _diag_blocks(w_k), _diag_blocks(w_v),
          _diag_blocks(w_q, transpose=True), _diag_blocks(w_v, transpose=True), wgate, bgate)
    sgub = jnp.broadcast_to(sgu_b[..., None], sgu_b.shape + (HEAD_DIM,))
    mhn = jnp.broadcast_to(mh_norm[..., None], mh_norm.shape + (CHUNK,))
    gfin = final_norm.reshape(1, d)

    xt = x.reshape(batch * seq, d)
    for l in range(depth):
        h, ug, vn, xm, xcr, sg = _ffn_in(xt, l, *f1, row3(mix_norm), bf(w_in), sgu_norm, conv_w, convb)
        hbt, *shared = _mix_bwd(xm, xcr, l, batch, mw)
        h2 = _mix_fwd(ug, vn, sg, h, hbt, shared, l, batch, bf(sgu_w), sgub, mhn,
                      row3(mlstm_skip), bf(w_out))
        xt = _ffn_out(h2, l, *f2, gfin, final_norm=(l == depth - 1))
    return xt.reshape(batch, seq, d)
```

```python
import functools

import jax
import jax.numpy as jnp
from jax import lax
from jax.experimental import pallas as pl
from jax.experimental.pallas import tpu as pltpu

F32 = jnp.float32
BF16 = jnp.bfloat16

EPS = 1e-6
HEADS = 4
HEAD_DIM = 128
CHUNK = 128
GROUP_WIDTH = HEADS * HEAD_DIM
CONV_WIDTH = 5
QKV_BLOCK = 4
MXU_WIDTH = 256
GATE_LANES = 128
IG_LANE = {False: 0, True: 2 * HEADS}
B_LANE = {False: HEADS, True: 3 * HEADS}
STATE_ROWS = HEAD_DIM + 16
HALO_ROWS = 8

FFN_TILE = 512
MIX_TILE = 512
FFN_HIDDEN_CHUNK = 256
VMEM_LIMIT_BYTES = 56 * 1024 * 1024

_NT = (((1,), (1,)), ((), ()))


def _rms_norm(x, g):
    return x * lax.rsqrt(jnp.mean(x * x, axis=-1, keepdims=True) + EPS) * g


def _layer_norm(x, g):
    mu = jnp.mean(x, axis=-1, keepdims=True)
    xc = x - mu
    var = jnp.mean(xc * xc, axis=-1, keepdims=True)
    return xc * lax.rsqrt(var + EPS) * g


def _log_sigmoid(x):
    return jnp.minimum(x, 0.0) - jnp.log1p(jnp.exp(-jnp.abs(x)))


def _swiglu(xn, wg_ref, wu_ref, wd_ref):
    hidden = wg_ref.shape[1]
    acc = None
    for lo in range(0, hidden, FFN_HIDDEN_CHUNK):
        hi = min(lo + FFN_HIDDEN_CHUNK, hidden)
        gate = jnp.dot(xn, wg_ref[:, lo:hi], preferred_element_type=F32)
        up = jnp.dot(xn, wu_ref[:, lo:hi], preferred_element_type=F32)
        act = (jax.nn.silu(gate) * up).astype(BF16)
        part = jnp.dot(act, wd_ref[lo:hi, :], preferred_element_type=F32)
        acc = part if acc is None else acc + part
    return acc


def _short_conv(rows, convw_ref, convb_ref):
    n = rows.shape[0]
    conv = None
    for tap in range(CONV_WIDTH):
        off = tap - CONV_WIDTH // 2
        shifted = rows if off == 0 else pltpu.roll(rows, (n - off) % n, axis=0)
        term = shifted * convw_ref[tap:tap + 1, :]
        conv = term if conv is None else conv + term
    return jax.nn.silu(conv + convb_ref[...])


def _ffn_in_kernel(x_ref, g1_ref, wg_ref, wu_ref, wd_ref, g2_ref, win_ref, sgun_ref, convw_ref, convb_ref,
                   h_ref, ug_ref, vn_ref, xm_ref, xcr_ref, sg_ref):
    x = x_ref[...]
    xn = _rms_norm(x, g1_ref[...]).astype(BF16)
    h = x + 0.5 * _swiglu(xn, wg_ref, wu_ref, wd_ref)
    h_ref[...] = h
    hn = _rms_norm(h, g2_ref[...]).astype(BF16)
    z = jnp.dot(hn, win_ref[...], preferred_element_type=F32)
    gw = GROUP_WIDTH
    ug_ref[...] = jax.nn.gelu(z[:, :gw])
    for hd in range(HEADS):
        cols = slice(hd * HEAD_DIM, (hd + 1) * HEAD_DIM)
        v = z[:, gw + hd * HEAD_DIM:gw + (hd + 1) * HEAD_DIM]
        vn_ref[:, cols] = _layer_norm(jax.nn.gelu(v), sgun_ref[hd:hd + 1, :]).astype(BF16)
    xm = z[:, 2 * gw:3 * gw]
    xm_ref[...] = xm
    xcr_ref[...] = _short_conv(xm, convw_ref, convb_ref)
    sg_ref[...] = jax.nn.sigmoid(z[:, 3 * gw:])


def _block_diag_dot(x, bd_ref):
    parts = [jnp.dot(x[:, i * MXU_WIDTH:(i + 1) * MXU_WIDTH], bd_ref[i], preferred_element_type=F32)
             for i in range(bd_ref.shape[0])]
    return jnp.concatenate(parts, axis=1)


def _mlstm_prepare(first, last, xm_ref, xprev_ref, xnext_ref, xcr_ref, convw_ref, convb_ref,
                   bdq_ref, bdk_ref, bdv_ref, bdqt_ref, bdvt_ref, wgate_ref, bgate_ref,
                   xc_ref, qt_ref, k_ref, vt_ref, ga_ref, gr_ref):
    tile = xm_ref.shape[0]
    xm = xm_ref[...]
    prev = jnp.where(first, 0.0, xprev_ref[...])
    nxt = jnp.where(last, 0.0, xnext_ref[...])
    r = HALO_ROWS

    def edge_conv(above, rows, below):
        ext = jnp.concatenate([above, rows, below], axis=0)
        return _short_conv(ext, convw_ref, convb_ref)[r:2 * r]

    xc = jnp.concatenate([edge_conv(prev, xm[0:r], xm[r:2 * r]),
                          xcr_ref[r:tile - r, :],
                          edge_conv(xm[tile - 2 * r:tile - r], xm[tile - r:], nxt)], axis=0)
    xc_ref[...] = xc

    xc_b = xc.astype(BF16)
    xm_b = xm.astype(BF16)
    q = _block_diag_dot(xc_b, bdq_ref).astype(BF16)
    k = (_block_diag_dot(xc_b, bdk_ref) * (HEAD_DIM ** -0.5)).astype(BF16)
    v = _block_diag_dot(xm_b, bdv_ref).astype(BF16)
    k_ref[...] = k
    for i in range(bdvt_ref.shape[0]):
        blk = slice(i * MXU_WIDTH, (i + 1) * MXU_WIDTH)
        qt = lax.dot_general(bdqt_ref[i], xc_b[:, blk], _NT, preferred_element_type=F32).astype(BF16)
        vt = lax.dot_general(bdvt_ref[i], xm_b[:, blk], _NT, preferred_element_type=F32)
        for c in range(tile // CHUNK):
            qt_ref[c, blk, :] = qt[:, c * CHUNK:(c + 1) * CHUNK]
            vt_ref[c, blk, :] = vt[:, c * CHUNK:(c + 1) * CHUNK]

    gates = (jnp.dot(q, wgate_ref[0:GROUP_WIDTH, :], preferred_element_type=F32)
             + jnp.dot(k, wgate_ref[GROUP_WIDTH:2 * GROUP_WIDTH, :], preferred_element_type=F32)
             + jnp.dot(v, wgate_ref[2 * GROUP_WIDTH:, :], preferred_element_type=F32)
             + bgate_ref[...])
    logf = _log_sigmoid(gates)
    lf_hi = logf.astype(BF16)
    rem = logf - lf_hi.astype(F32)
    lf_mid = rem.astype(BF16)
    lf_lo = (rem - lf_mid.astype(F32)).astype(BF16)

    row = lax.broadcasted_iota(jnp.int32, (CHUNK, CHUNK), 0)
    col = lax.broadcasted_iota(jnp.int32, (CHUNK, CHUNK), 1)
    prefix_fwd = (col <= row).astype(BF16)
    prefix_bwd = (col >= row).astype(BF16)
    lane = lax.broadcasted_iota(jnp.int32, (CHUNK, GATE_LANES), 1)
    is_b_fwd = (lane >= B_LANE[False]) & (lane < B_LANE[False] + HEADS)
    is_b_bwd = (lane >= B_LANE[True]) & (lane < B_LANE[True] + HEADS)

    def prefix_sum(prefix, rows):
        return (jnp.dot(prefix, lf_hi[rows], preferred_element_type=F32)
                + jnp.dot(prefix, lf_mid[rows], preferred_element_type=F32)
                + jnp.dot(prefix, lf_lo[rows], preferred_element_type=F32))

    for c in range(tile // CHUNK):
        rows = slice(c * CHUNK, (c + 1) * CHUNK)
        gc = jnp.where(is_b_fwd, prefix_sum(prefix_fwd, rows),
                       jnp.where(is_b_bwd, prefix_sum(prefix_bwd, rows), gates[rows]))
        gr_ref[rows, :] = gc.T
        ga_ref[rows, :] = gc - pltpu.roll(gc, GATE_LANES - HEADS, axis=1)


def _mlstm_direction(reverse, n_chunks, qt_ref, k_ref, vt_ref, ga_ref, gr_ref, state_s, m_s, out_ref):
    key = lax.broadcasted_iota(jnp.int32, (CHUNK, CHUNK), 0)
    qry = lax.broadcasted_iota(jnp.int32, (CHUNK, CHUNK), 1)
    visible = (key >= qry) if reverse else (key <= qry)
    first_row = lax.broadcasted_iota(jnp.int32, (STATE_ROWS - HEAD_DIM, CHUNK), 0) == 0
    edge = 0 if reverse else CHUNK - 1
    ig0, b0 = IG_LANE[reverse], B_LANE[reverse]

    order = [(n_chunks - 1 - i) if reverse else i for i in range(n_chunks)]
    head_cols = [slice(h * HEAD_DIM, (h + 1) * HEAD_DIM) for h in range(HEADS)]
    chunk_rows = [slice(c * CHUNK, (c + 1) * CHUNK) for c in range(n_chunks)]
    gr = [gr_ref[chunk_rows[c], :] for c in range(n_chunks)]
    b_rows = {(c, h): gr[c][b0 + h:b0 + h + 1, :] for c in order for h in range(HEADS)}

    incr = {}
    for c in order:
        for h in range(HEADS):
            b_row = b_rows[c, h]
            ig_row = gr[c][ig0 + h:ig0 + h + 1, :]
            b_edge = jnp.broadcast_to(b_row[:, edge:edge + 1], b_row.shape)
            dec = (b_edge - b_row) + ig_row
            dec_max = jnp.broadcast_to(jnp.max(dec, axis=1, keepdims=True), dec.shape)
            wk = jnp.exp(dec - dec_max)
            lhs = jnp.concatenate([vt_ref[c, head_cols[h], :] * wk, jnp.where(first_row, wk, 0.0)],
                                  axis=0).astype(BF16)
            update = jnp.dot(lhs, k_ref[chunk_rows[c], head_cols[h]], preferred_element_type=F32)
            incr[c, h] = (b_edge, dec_max, update)

    states = [state_s[h] for h in range(HEADS)]
    ms = [m_s[h:h + 1, :] for h in range(HEADS)]
    partial = {}
    for c in order:
        ga = ga_ref[chunk_rows[c], :]
        for h in range(HEADS):
            b_row = b_rows[c, h]
            a_col = ga[:, ig0 + h:ig0 + h + 1]
            state, m_prev = states[h], ms[h]
            d = jnp.where(visible, b_row + a_col, -jnp.inf)
            m_loc = jnp.max(d, axis=0, keepdims=True)
            lhs = jnp.concatenate([k_ref[chunk_rows[c], head_cols[h]], state.astype(BF16)], axis=0)
            both = jnp.dot(lhs, qt_ref[c, head_cols[h], :], preferred_element_type=F32)
            st = both[:CHUNK] * jnp.exp(d - m_loc)
            carried = both[CHUNK:]
            inter = b_row + m_prev
            m_j = jnp.maximum(inter, m_loc)
            w_inter = jnp.exp(inter - m_j)
            w_loc = jnp.exp(m_loc - m_j)
            nq = (w_inter * carried[HEAD_DIM:HEAD_DIM + 1]
                  + w_loc * jnp.sum(st, axis=0, keepdims=True))
            inv = 1.0 / jnp.maximum(jnp.abs(nq), jnp.exp(-m_j))
            partial[c, h] = (st.astype(BF16), w_inter * inv, w_loc * inv, carried[:HEAD_DIM])

            b_edge, dec_max, update = incr[c, h]
            m_new = jnp.maximum(b_edge + m_prev, dec_max)
            states[h] = (jnp.exp(b_edge + m_prev - m_new) * state
                         + jnp.exp(dec_max - m_new) * update)
            ms[h] = m_new
    for h in range(HEADS):
        state_s[h] = states[h]
        m_s[h:h + 1, :] = ms[h]

    for c in order:
        for h in range(HEADS):
            st_b, s_inter, s_loc, carried = partial[c, h]
            num_loc = jnp.dot(vt_ref[c, head_cols[h], :].astype(BF16), st_b, preferred_element_type=F32)
            out_ref[c, head_cols[h], :] = s_inter * carried + s_loc * num_loc


def _reset_state(state_s, m_s):
    state_s[...] = jnp.zeros_like(state_s)
    m_s[...] = jnp.zeros_like(m_s)


def _mix_bwd_kernel(n_tiles, xm_ref, xprev_ref, xnext_ref, xcr_ref, convw_ref, convb_ref, bdq_ref, bdk_ref,
                    bdv_ref, bdqt_ref, bdvt_ref, wgate_ref, bgate_ref,
                    hbt_ref, xc_ref, qt_ref, k_ref, vt_ref, ga_ref, gr_ref, state_s, m_s):
    j = pl.program_id(1)
    seq_tile = n_tiles - 1 - j

    @pl.when(j == 0)
    def _():
        _reset_state(state_s, m_s)

    _mlstm_prepare(seq_tile == 0, seq_tile == n_tiles - 1, xm_ref, xprev_ref, xnext_ref, xcr_ref,
                   convw_ref, convb_ref, bdq_ref, bdk_ref, bdv_ref, bdqt_ref, bdvt_ref, wgate_ref,
                   bgate_ref, xc_ref, qt_ref, k_ref, vt_ref, ga_ref, gr_ref)
    _mlstm_direction(True, xm_ref.shape[0] // CHUNK, qt_ref, k_ref, vt_ref, ga_ref, gr_ref,
                     state_s, m_s, hbt_ref)


def _mix_fwd_kernel(final_norm, ug_ref, vn_ref, sg_ref, h_ref, hbt_ref, xc_ref, qt_ref, k_ref, vt_ref,
                    ga_ref, gr_ref, sguw_ref, sgub_ref, mhn_ref, skip_ref, wout_ref,
                    g2_ref, wg_ref, wu_ref, wd_ref, gf_ref, o_ref,
                    state_s, m_s, hft_s, y_s):
    tile = h_ref.shape[0]

    @pl.when(pl.program_id(1) == 0)
    def _():
        _reset_state(state_s, m_s)

    _mlstm_direction(False, tile // CHUNK, qt_ref, k_ref, vt_ref, ga_ref, gr_ref, state_s, m_s, hft_s)

    for h in range(HEADS):
        cols = slice(h * HEAD_DIM, (h + 1) * HEAD_DIM)
        for c in range(tile // CHUNK):
            rows = slice(c * CHUNK, (c + 1) * CHUNK)
            mixed = jnp.dot(sguw_ref[h], vn_ref[rows, cols], preferred_element_type=F32) + sgub_ref[h]
            y_s[rows, cols] = (ug_ref[rows, cols] * mixed).astype(BF16)

    for c in range(tile // CHUNK):
        rows = slice(c * CHUNK, (c + 1) * CHUNK)
        for h in range(HEADS):
            cols = slice(h * HEAD_DIM, (h + 1) * HEAD_DIM)
            ht = hft_s[c, cols, :] + hbt_ref[c, cols, :]
            mu = jnp.mean(ht, axis=0, keepdims=True)
            hc = ht - mu
            var = jnp.mean(hc * hc, axis=0, keepdims=True)
            hn = (hc * lax.rsqrt(var + EPS) * mhn_ref[h]).T
            ym = (hn + skip_ref[:, cols] * xc_ref[rows, cols]) * sg_ref[rows, cols]
            y_s[rows, GROUP_WIDTH + h * HEAD_DIM:GROUP_WIDTH + (h + 1) * HEAD_DIM] = ym.astype(BF16)

    h = h_ref[...] + jnp.dot(y_s[...], wout_ref[...], preferred_element_type=F32)

    hn = _rms_norm(h, g2_ref[...]).astype(BF16)
    x = h + 0.5 * _swiglu(hn, wg_ref, wu_ref, wd_ref)
    if final_norm:
        x = _rms_norm(x, gf_ref[...])
    o_ref[...] = x


def _compiler_params(grid_rank):
    return pltpu.CompilerParams(dimension_semantics=("arbitrary",) * grid_rank,
                                vmem_limit_bytes=VMEM_LIMIT_BYTES)


def _resident(shape, layer=None):
    nd = len(shape)
    if layer is None:
        return pl.BlockSpec(shape, lambda *_: (0,) * nd, pipeline_mode=pl.Buffered(1))
    return pl.BlockSpec((None,) + shape, lambda *_: (layer,) + (0,) * nd,
                        pipeline_mode=pl.Buffered(1))


def _ffn_in(x, l, g1, wg, wu, wd, g2, win, sgun, convw, convb):
    tokens, d = x.shape
    hidden = wg.shape[-1]
    zw = win.shape[-1]
    gw = GROUP_WIDTH
    assert zw == 4 * gw
    row = lambda i: (i, 0)
    group_spec = pl.BlockSpec((FFN_TILE, gw), row)
    group_f32 = jax.ShapeDtypeStruct((tokens, gw), F32)
    return pl.pallas_call(
        _ffn_in_kernel,
        grid=(tokens // FFN_TILE,),
        in_specs=[pl.BlockSpec((FFN_TILE, d), row),
                  _resident((1, d), l), _resident((d, hidden), l), _resident((d, hidden), l),
                  _resident((hidden, d), l), _resident((1, d), l), _resident((d, zw), l),
                  _resident((HEADS, HEAD_DIM), l), _resident((CONV_WIDTH, gw), l), _resident((1, gw), l)],
        out_specs=[pl.BlockSpec((FFN_TILE, d), row)] + [group_spec] * 5,
        out_shape=[jax.ShapeDtypeStruct((tokens, d), F32), group_f32,
                   jax.ShapeDtypeStruct((tokens, gw), BF16), group_f32, group_f32, group_f32],
        compiler_params=_compiler_params(1),
        name=f"ffn_in_{l}",
    )(x, g1, wg, wu, wd, g2, win, sgun, convw, convb)


def _state_scratch():
    return [pltpu.VMEM((HEADS, STATE_ROWS, HEAD_DIM), F32),
            pltpu.VMEM((8, 128), F32)]


def _tile_row(n_tiles, reverse):
    return lambda b, j: b * n_tiles + ((n_tiles - 1 - j) if reverse else j)


def _shared_specs(tile_row):
    gw = GROUP_WIDTH
    chunks = MIX_TILE // CHUNK
    rows2 = lambda b, j: (tile_row(b, j), 0)
    rows3 = lambda b, j: (tile_row(b, j), 0, 0)
    return [pl.BlockSpec((MIX_TILE, gw), rows2),
            pl.BlockSpec((chunks, gw, CHUNK), rows3),
            pl.BlockSpec((MIX_TILE, gw), rows2),
            pl.BlockSpec((chunks, gw, CHUNK), rows3),
            pl.BlockSpec((MIX_TILE, GATE_LANES), rows2),
            pl.BlockSpec((MIX_TILE, CHUNK), rows2)]


def _mix_bwd(xm, xcr, l, batch, mw):
    tokens = xm.shape[0]
    n_tiles = tokens // batch // MIX_TILE
    tile_row = _tile_row(n_tiles, True)
    gw = GROUP_WIDTH
    halo_per_tile = MIX_TILE // HALO_ROWS
    last_halo = tokens // HALO_ROWS - 1
    n_diag = gw // MXU_WIDTH
    chunks = MIX_TILE // CHUNK
    shared_shapes = [jax.ShapeDtypeStruct((tokens, gw), F32),
                     jax.ShapeDtypeStruct((tokens // CHUNK, gw, CHUNK), BF16),
                     jax.ShapeDtypeStruct((tokens, gw), BF16),
                     jax.ShapeDtypeStruct((tokens // CHUNK, gw, CHUNK), F32),
                     jax.ShapeDtypeStruct((tokens, GATE_LANES), F32),
                     jax.ShapeDtypeStruct((tokens, CHUNK), F32)]
    return pl.pallas_call(
        functools.partial(_mix_bwd_kernel, n_tiles),
        grid=(batch, n_tiles),
        in_specs=[pl.BlockSpec((MIX_TILE, gw), lambda b, j: (tile_row(b, j), 0)),
                  pl.BlockSpec((HALO_ROWS, gw), lambda b, j: (
                      jnp.maximum(tile_row(b, j) * halo_per_tile - 1, 0), 0)),
                  pl.BlockSpec((HALO_ROWS, gw), lambda b, j: (
                      jnp.minimum((tile_row(b, j) + 1) * halo_per_tile, last_halo), 0)),
                  pl.BlockSpec((MIX_TILE, gw), lambda b, j: (tile_row(b, j), 0)),
                  _resident((CONV_WIDTH, gw), l), _resident((1, gw), l),
                  _resident((n_diag, MXU_WIDTH, MXU_WIDTH), l),
                  _resident((n_diag, MXU_WIDTH, MXU_WIDTH), l),
                  _resident((n_diag, MXU_WIDTH, MXU_WIDTH), l),
                  _resident((n_diag, MXU_WIDTH, MXU_WIDTH), l),
                  _resident((n_diag, MXU_WIDTH, MXU_WIDTH), l),
                  _resident((3 * gw, GATE_LANES), l), _resident((1, GATE_LANES), l)],
        out_specs=[pl.BlockSpec((chunks, gw, CHUNK), lambda b, j: (tile_row(b, j), 0, 0))]
                  + _shared_specs(tile_row),
        out_shape=[jax.ShapeDtypeStruct((tokens // CHUNK, gw, CHUNK), F32)] + shared_shapes,
        scratch_shapes=_state_scratch(),
        compiler_params=_compiler_params(2),
        name=f"mix_bwd_{l}",
    )(xm, xm, xm, xcr, *mw)


def _mix_fwd(ug, vn, sg, h, hbt, shared, l, batch, sguw, sgub, mhn, skip, wout,
             g2, wg, wu, wd, gf, final_norm):
    tokens, d = h.shape
    hidden = wg.shape[-1]
    n_tiles = tokens // batch // MIX_TILE
    tile_row = _tile_row(n_tiles, False)
    gw = GROUP_WIDTH
    chunks = MIX_TILE // CHUNK
    rows2 = lambda b, j: (tile_row(b, j), 0)
    return pl.pallas_call(
        functools.partial(_mix_fwd_kernel, final_norm),
        grid=(batch, n_tiles),
        in_specs=[pl.BlockSpec((MIX_TILE, gw), rows2),
                  pl.BlockSpec((MIX_TILE, gw), rows2),
                  pl.BlockSpec((MIX_TILE, gw), rows2),
                  pl.BlockSpec((MIX_TILE, d), rows2),
                  pl.BlockSpec((chunks, gw, CHUNK), lambda b, j: (tile_row(b, j), 0, 0))]
                 + _shared_specs(tile_row)
                 + [_resident((HEADS, CHUNK, CHUNK), l),
                    _resident((HEADS, CHUNK, HEAD_DIM), l), _resident((HEADS, HEAD_DIM, CHUNK), l),
                    _resident((1, gw), l), _resident((2 * gw, d), l),
                    _resident((1, d), l), _resident((d, hidden), l), _resident((d, hidden), l),
                    _resident((hidden, d), l), _resident((1, d))],
        out_specs=pl.BlockSpec((MIX_TILE, d), rows2),
        out_shape=jax.ShapeDtypeStruct((tokens, d), F32),
        scratch_shapes=_state_scratch() + [pltpu.VMEM((chunks, gw, CHUNK), F32),
                                           pltpu.VMEM((MIX_TILE, 2 * gw), BF16)],
        compiler_params=_compiler_params(2),
        name=f"mix_fwd_{l}",
    )(ug, vn, sg, h, hbt, *shared, sguw, sgub, mhn, skip, wout, g2, wg, wu, wd, gf)


def _diag_blocks(w, transpose=False):
    depth, groups, bi, bo = w.shape
    rows = w.reshape(depth, groups * bi // MXU_WIDTH, MXU_WIDTH, bo)
    idx = jnp.arange(MXU_WIDTH)
    tiled = jnp.einsum('dnro,oc->dnrc', rows, (idx[None, :] % bo == jnp.arange(bo)[:, None]).astype(F32),
                       precision=lax.Precision.HIGHEST)
    dense = jnp.where(idx[:, None] // bi == idx[None, :] // bo, tiled, 0.0)
    if transpose:
        dense = jnp.swapaxes(dense, -1, -2)
    return dense.astype(BF16)


def _pack_gates(w_fwd, b_fwd, w_bwd, b_bwd):
    depth, rows, n = w_fwd.shape
    w = jnp.concatenate([w_fwd, w_bwd, jnp.zeros((depth, rows, GATE_LANES - 2 * n), F32)], axis=-1)
    b = jnp.concatenate([b_fwd, b_bwd, jnp.zeros((depth, GATE_LANES - 2 * n), F32)], axis=-1)
    return w.astype(BF16), b.reshape(depth, 1, GATE_LANES)


def kernel(x, ffn1_norm, ffn1_w_gate, ffn1_w_up, ffn1_w_down, mix_norm, w_in, sgu_norm, sgu_w, sgu_b, conv_w, conv_b, w_q, w_k, w_v, gate_w_fwd, gate_b_fwd, gate_w_bwd, gate_b_bwd, mh_norm, mlstm_skip, w_out, ffn2_norm, ffn2_w_gate, ffn2_w_up, ffn2_w_down, final_norm):
    batch, seq, d = x.shape
    depth = w_in.shape[0]
    assert seq % MIX_TILE == 0 and (batch * seq) % FFN_TILE == 0
    assert sgu_w.shape[1:] == (HEADS, CHUNK, CHUNK) and mh_norm.shape[1:] == (HEADS, HEAD_DIM)
    assert gate_w_fwd.shape[1:] == (3 * GROUP_WIDTH, 2 * HEADS)

    bf = lambda w: w.astype(BF16)
    row3 = lambda g: g.reshape(depth, 1, -1)
    f1 = (row3(ffn1_norm), bf(ffn1_w_gate), bf(ffn1_w_up), bf(ffn1_w_down))
    f2 = (row3(ffn2_norm), bf(ffn2_w_gate), bf(ffn2_w_up), bf(ffn2_w_down))
    wgate, bgate = _pack_gates(gate_w_fwd, gate_b_fwd, gate_w_bwd, gate_b_bwd)
    convb = row3(conv_b)
    mw = (conv_w, convb, _diag_blocks(w_q), _diag_blocks(w_k), _diag_blocks(w_v),
          _diag_blocks(w_q, transpose=True), _diag_blocks(w_v, transpose=True), wgate, bgate)
    sgub = jnp.broadcast_to(sgu_b[..., None], sgu_b.shape + (HEAD_DIM,))
    mhn = jnp.broadcast_to(mh_norm[..., None], mh_norm.shape + (CHUNK,))
    gfin = final_norm.reshape(1, d)

    xt = x.reshape(batch * seq, d)
    for l in range(depth):
        h, ug, vn, xm, xcr, sg = _ffn_in(xt, l, *f1, row3(mix_norm), bf(w_in), sgu_norm, conv_w, convb)
        hbt, *shared = _mix_bwd(xm, xcr, l, batch, mw)
        xt = _mix_fwd(ug, vn, sg, h, hbt, shared, l, batch, bf(sgu_w), sgub, mhn,
                      row3(mlstm_skip), bf(w_out), *f2, gfin, final_norm=(l == depth - 1))
    return xt.reshape(batch, seq, d)
```

```python
import functools

import jax
import jax.numpy as jnp
from jax import lax
from jax.experimental import pallas as pl
from jax.experimental.pallas import tpu as pltpu

F32 = jnp.float32
BF16 = jnp.bfloat16

EPS = 1e-6
HEADS = 4
HEAD_DIM = 128
CHUNK = 128
GROUP_WIDTH = HEADS * HEAD_DIM
CONV_WIDTH = 5
QKV_BLOCK = 4
MXU_WIDTH = 256
GATE_LANES = 128
GATE_ROWS = 4 * HEADS
IG_LANE = {False: 0, True: 2 * HEADS}
B_LANE = {False: HEADS, True: 3 * HEADS}
STATE_ROWS = HEAD_DIM + 16
HALO_ROWS = 8

FFN_TILE = 512
MIX_TILE = 512
FFN_HIDDEN_CHUNK = 256
VMEM_LIMIT_BYTES = 56 * 1024 * 1024

_NT = (((1,), (1,)), ((), ()))


def _rms_norm(x, g):
    return x * lax.rsqrt(jnp.mean(x * x, axis=-1, keepdims=True) + EPS) * g


def _layer_norm(x, g):
    mu = jnp.mean(x, axis=-1, keepdims=True)
    xc = x - mu
    var = jnp.mean(xc * xc, axis=-1, keepdims=True)
    return xc * lax.rsqrt(var + EPS) * g


def _gelu_tanh(x):
    c2 = 2.0 * (2.0 / jnp.pi) ** 0.5 * 1.4426950408889634
    return x / (1.0 + jnp.exp2(x * (x * x * (-0.044715 * c2) - c2)))


def _log_sigmoid(x):
    return jnp.minimum(x, 0.0) - jnp.log1p(jnp.exp(-jnp.abs(x)))


def _swiglu(xn, wg_ref, wu_ref, wd_ref):
    hidden = wg_ref.shape[1]
    acc = None
    for lo in range(0, hidden, FFN_HIDDEN_CHUNK):
        hi = min(lo + FFN_HIDDEN_CHUNK, hidden)
        gate = jnp.dot(xn, wg_ref[:, lo:hi], preferred_element_type=F32)
        up = jnp.dot(xn, wu_ref[:, lo:hi], preferred_element_type=F32)
        act = (jax.nn.silu(gate) * up).astype(BF16)
        part = jnp.dot(act, wd_ref[lo:hi, :], preferred_element_type=F32)
        acc = part if acc is None else acc + part
    return acc


def _short_conv(rows, convw_ref, convb_ref):
    n = rows.shape[0]
    conv = None
    for tap in range(CONV_WIDTH):
        off = tap - CONV_WIDTH // 2
        shifted = rows if off == 0 else pltpu.roll(rows, (n - off) % n, axis=0)
        term = shifted * convw_ref[tap:tap + 1, :]
        conv = term if conv is None else conv + term
    return jax.nn.silu(conv + convb_ref[...])


def _ffn_in_kernel(x_ref, g1_ref, wg_ref, wu_ref, wd_ref, g2_ref, win_ref, sgun_ref, convw_ref, convb_ref,
                   h_ref, ug_ref, vn_ref, xm_ref, xcr_ref, sg_ref):
    x = x_ref[...]
    xn = _rms_norm(x, g1_ref[...]).astype(BF16)
    h = x + 0.5 * _swiglu(xn, wg_ref, wu_ref, wd_ref)
    h_ref[...] = h
    hn = _rms_norm(h, g2_ref[...]).astype(BF16)
    z = jnp.dot(hn, win_ref[...], preferred_element_type=F32)
    gw = GROUP_WIDTH
    ug_ref[...] = _gelu_tanh(z[:, :gw])
    for hd in range(HEADS):
        cols = slice(hd * HEAD_DIM, (hd + 1) * HEAD_DIM)
        v = z[:, gw + hd * HEAD_DIM:gw + (hd + 1) * HEAD_DIM]
        vn_ref[:, cols] = _layer_norm(_gelu_tanh(v), sgun_ref[hd:hd + 1, :]).astype(BF16)
    xm = z[:, 2 * gw:3 * gw]
    xm_ref[...] = xm
    xcr_ref[...] = _short_conv(xm, convw_ref, convb_ref)
    sg_ref[...] = jax.nn.sigmoid(z[:, 3 * gw:])


def _block_diag_dot(x, bd_ref):
    parts = [jnp.dot(x[:, i * MXU_WIDTH:(i + 1) * MXU_WIDTH], bd_ref[i], preferred_element_type=F32)
             for i in range(bd_ref.shape[0])]
    return jnp.concatenate(parts, axis=1)


def _mlstm_prepare(first, last, xm_ref, xprev_ref, xnext_ref, xcr_ref, convw_ref, convb_ref,
                   bdq_ref, bdk_ref, bdv_ref, bdqt_ref, bdvt_ref, wgate_ref, bgate_ref,
                   xce_ref, qt_ref, k_ref, vt_ref, ga_ref, gr_ref):
    tile = xm_ref.shape[0]
    xm = xm_ref[...]
    prev = jnp.where(first, 0.0, xprev_ref[...])
    nxt = jnp.where(last, 0.0, xnext_ref[...])
    r = HALO_ROWS

    def edge_conv(above, rows, below):
        ext = jnp.concatenate([above, rows, below], axis=0)
        return _short_conv(ext, convw_ref, convb_ref)[r:2 * r]

    top = edge_conv(prev, xm[0:r], xm[r:2 * r])
    bottom = edge_conv(xm[tile - 2 * r:tile - r], xm[tile - r:], nxt)
    xce_ref[0:r, :] = top
    xce_ref[r:, :] = bottom
    xc = jnp.concatenate([top, xcr_ref[r:tile - r, :], bottom], axis=0)

    xc_b = xc.astype(BF16)
    xm_b = xm.astype(BF16)
    q = _block_diag_dot(xc_b, bdq_ref).astype(BF16)
    k = (_block_diag_dot(xc_b, bdk_ref) * (HEAD_DIM ** -0.5)).astype(BF16)
    v = _block_diag_dot(xm_b, bdv_ref).astype(BF16)
    k_ref[...] = k
    for i in range(bdvt_ref.shape[0]):
        blk = slice(i * MXU_WIDTH, (i + 1) * MXU_WIDTH)
        qt = lax.dot_general(bdqt_ref[i], xc_b[:, blk], _NT, preferred_element_type=F32).astype(BF16)
        vt = lax.dot_general(bdvt_ref[i], xm_b[:, blk], _NT, preferred_element_type=F32)
        for c in range(tile // CHUNK):
            qt_ref[c, blk, :] = qt[:, c * CHUNK:(c + 1) * CHUNK]
            vt_ref[c, blk, :] = vt[:, c * CHUNK:(c + 1) * CHUNK]

    gates = (jnp.dot(q, wgate_ref[0:GROUP_WIDTH, :], preferred_element_type=F32)
             + jnp.dot(k, wgate_ref[GROUP_WIDTH:2 * GROUP_WIDTH, :], preferred_element_type=F32)
             + jnp.dot(v, wgate_ref[2 * GROUP_WIDTH:, :], preferred_element_type=F32)
             + bgate_ref[...])
    n_chunks = tile // CHUNK
    lane = lax.broadcasted_iota(jnp.int32, gates.shape, 1)
    is_forget = (((lane >= B_LANE[False]) & (lane < B_LANE[False] + HEADS))
                 | ((lane >= B_LANE[True]) & (lane < B_LANE[True] + HEADS)))
    mixed = jnp.where(is_forget, _log_sigmoid(gates), gates)
    gt = jnp.concatenate([mixed[c * CHUNK:(c + 1) * CHUNK].T[:GATE_ROWS] for c in range(n_chunks)], axis=0)
    n_rows = n_chunks * GATE_ROWS
    hi = gt.astype(BF16)
    rem = gt - hi.astype(F32)
    mid = rem.astype(BF16)
    pieces = jnp.concatenate([hi, mid, (rem - mid.astype(F32)).astype(BF16)], axis=0)
    src = lax.broadcasted_iota(jnp.int32, (CHUNK, CHUNK), 0)
    dst = lax.broadcasted_iota(jnp.int32, (CHUNK, CHUNK), 1)

    def prefix_sum(includes):
        parts = jnp.dot(pieces, includes.astype(BF16), preferred_element_type=F32)
        return parts[:n_rows] + parts[n_rows:2 * n_rows] + parts[2 * n_rows:]

    row = lax.broadcasted_iota(jnp.int32, gt.shape, 0) % GATE_ROWS
    gr = jnp.where((row >= B_LANE[False]) & (row < B_LANE[False] + HEADS), prefix_sum(src <= dst),
                   jnp.where((row >= B_LANE[True]) & (row < B_LANE[True] + HEADS), prefix_sum(src >= dst), gt))
    gr_ref[...] = gr
    a_rows = gr - pltpu.roll(gr, n_rows - HEADS, axis=0)
    pad = jnp.zeros((GATE_LANES - GATE_ROWS, CHUNK), F32)
    for c in range(n_chunks):
        a_c = jnp.concatenate([a_rows[c * GATE_ROWS:(c + 1) * GATE_ROWS], pad], axis=0)
        ga_ref[c * CHUNK:(c + 1) * CHUNK, :] = a_c.T


def _mlstm_direction(reverse, n_chunks, qt_ref, k_ref, vt_ref, ga_ref, gr_ref, state_s, m_s, out_ref):
    key = lax.broadcasted_iota(jnp.int32, (CHUNK, CHUNK), 0)
    qry = lax.broadcasted_iota(jnp.int32, (CHUNK, CHUNK), 1)
    visible = (key >= qry) if reverse else (key <= qry)
    first_row = lax.broadcasted_iota(jnp.int32, (STATE_ROWS - HEAD_DIM, CHUNK), 0) == 0
    edge = 0 if reverse else CHUNK - 1
    ig0, b0 = IG_LANE[reverse], B_LANE[reverse]

    order = [(n_chunks - 1 - i) if reverse else i for i in range(n_chunks)]
    head_cols = [slice(h * HEAD_DIM, (h + 1) * HEAD_DIM) for h in range(HEADS)]
    chunk_rows = [slice(c * CHUNK, (c + 1) * CHUNK) for c in range(n_chunks)]
    gr = [gr_ref[c * GATE_ROWS:(c + 1) * GATE_ROWS, :] for c in range(n_chunks)]
    b_rows = {(c, h): gr[c][b0 + h:b0 + h + 1, :] for c in order for h in range(HEADS)}

    incr = {}
    for c in order:
        for h in range(HEADS):
            b_row = b_rows[c, h]
            ig_row = gr[c][ig0 + h:ig0 + h + 1, :]
            b_edge = jnp.broadcast_to(b_row[:, edge:edge + 1], b_row.shape)
            dec = (b_edge - b_row) + ig_row
            dec_max = jnp.broadcast_to(jnp.max(dec, axis=1, keepdims=True), dec.shape)
            wk = jnp.exp(dec - dec_max)
            lhs = jnp.concatenate([vt_ref[c, head_cols[h], :] * wk, jnp.where(first_row, wk, 0.0)],
                                  axis=0).astype(BF16)
            update = jnp.dot(lhs, k_ref[chunk_rows[c], head_cols[h]], preferred_element_type=F32)
            incr[c, h] = (b_edge, dec_max, update)

    states = [state_s[h] for h in range(HEADS)]
    ms = [m_s[h:h + 1, :] for h in range(HEADS)]
    partial = {}
    for c in order:
        ga = ga_ref[chunk_rows[c], :]
        for h in range(HEADS):
            b_row = b_rows[c, h]
            a_col = ga[:, ig0 + h:ig0 + h + 1]
            state, m_prev = states[h], ms[h]
            d = jnp.where(visible, b_row + a_col, -jnp.inf)
            m_loc = jnp.max(d, axis=0, keepdims=True)
            lhs = jnp.concatenate([k_ref[chunk_rows[c], head_cols[h]], state.astype(BF16)], axis=0)
            both = jnp.dot(lhs, qt_ref[c, head_cols[h], :], preferred_element_type=F32)
            st = both[:CHUNK] * jnp.exp(d - m_loc)
            carried = both[CHUNK:]
            inter = b_row + m_prev
            m_j = jnp.maximum(inter, m_loc)
            w_inter = jnp.exp(inter - m_j)
            w_loc = jnp.exp(m_loc - m_j)
            nq = (w_inter * carried[HEAD_DIM:HEAD_DIM + 1]
                  + w_loc * jnp.sum(st, axis=0, keepdims=True))
            inv = 1.0 / jnp.maximum(jnp.abs(nq), jnp.exp(-m_j))
            partial[c, h] = (st.astype(BF16), w_inter * inv, w_loc * inv, carried[:HEAD_DIM])

            b_edge, dec_max, update = incr[c, h]
            m_new = jnp.maximum(b_edge + m_prev, dec_max)
            states[h] = (jnp.exp(b_edge + m_prev - m_new) * state
                         + jnp.exp(dec_max - m_new) * update)
            ms[h] = m_new
    for h in range(HEADS):
        state_s[h] = states[h]
        m_s[h:h + 1, :] = ms[h]

    for c in order:
        for h in range(HEADS):
            st_b, s_inter, s_loc, carried = partial[c, h]
            num_loc = jnp.dot(vt_ref[c, head_cols[h], :].astype(BF16), st_b, preferred_element_type=F32)
            out_ref[c, head_cols[h], :] = s_inter * carried + s_loc * num_loc


def _reset_state(state_s, m_s):
    state_s[...] = jnp.zeros_like(state_s)
    m_s[...] = jnp.zeros_like(m_s)


def _mix_bwd_kernel(n_tiles, xm_ref, xprev_ref, xnext_ref, xcr_ref, convw_ref, convb_ref, bdq_ref, bdk_ref,
                    bdv_ref, bdqt_ref, bdvt_ref, wgate_ref, bgate_ref,
                    hbt_ref, xce_ref, qt_ref, k_ref, vt_ref, ga_ref, gr_ref, state_s, m_s):
    j = pl.program_id(1)
    seq_tile = n_tiles - 1 - j

    @pl.when(j == 0)
    def _():
        _reset_state(state_s, m_s)

    _mlstm_prepare(seq_tile == 0, seq_tile == n_tiles - 1, xm_ref, xprev_ref, xnext_ref, xcr_ref,
                   convw_ref, convb_ref, bdq_ref, bdk_ref, bdv_ref, bdqt_ref, bdvt_ref, wgate_ref,
                   bgate_ref, xce_ref, qt_ref, k_ref, vt_ref, ga_ref, gr_ref)
    _mlstm_direction(True, xm_ref.shape[0] // CHUNK, qt_ref, k_ref, vt_ref, ga_ref, gr_ref,
                     state_s, m_s, hbt_ref)


def _mix_fwd_kernel(final_norm, ug_ref, vn_ref, sg_ref, xcr_ref, h_ref, hbt_ref, xce_ref, qt_ref, k_ref, vt_ref,
                    ga_ref, gr_ref, sguw_ref, sgub_ref, mhn_ref, skip_ref, wout_ref,
                    g2_ref, wg_ref, wu_ref, wd_ref, gf_ref, o_ref,
                    state_s, m_s, hft_s, y_s):
    tile = h_ref.shape[0]

    @pl.when(pl.program_id(1) == 0)
    def _():
        _reset_state(state_s, m_s)

    _mlstm_direction(False, tile // CHUNK, qt_ref, k_ref, vt_ref, ga_ref, gr_ref, state_s, m_s, hft_s)

    for h in range(HEADS):
        cols = slice(h * HEAD_DIM, (h + 1) * HEAD_DIM)
        for c in range(tile // CHUNK):
            rows = slice(c * CHUNK, (c + 1) * CHUNK)
            mixed = jnp.dot(sguw_ref[h], vn_ref[rows, cols], preferred_element_type=F32) + sgub_ref[h]
            y_s[rows, cols] = (ug_ref[rows, cols] * mixed).astype(BF16)

    n_chunks = tile // CHUNK
    for c in range(n_chunks):
        rows = slice(c * CHUNK, (c + 1) * CHUNK)
        for h in range(HEADS):
            cols = slice(h * HEAD_DIM, (h + 1) * HEAD_DIM)
            xc = xcr_ref[rows, cols]
            if c == 0:
                xc = jnp.concatenate([xce_ref[0:HALO_ROWS, cols], xc[HALO_ROWS:]], axis=0)
            if c == n_chunks - 1:
                xc = jnp.concatenate([xc[:CHUNK - HALO_ROWS], xce_ref[HALO_ROWS:, cols]], axis=0)
            ht = hft_s[c, cols, :] + hbt_ref[c, cols, :]
            mu = jnp.mean(ht, axis=0, keepdims=True)
            hc = ht - mu
            var = jnp.mean(hc * hc, axis=0, keepdims=True)
            hn = (hc * lax.rsqrt(var + EPS) * mhn_ref[h]).T
            ym = (hn + skip_ref[:, cols] * xc) * sg_ref[rows, cols]
            y_s[rows, GROUP_WIDTH + h * HEAD_DIM:GROUP_WIDTH + (h + 1) * HEAD_DIM] = ym.astype(BF16)

    h = h_ref[...] + jnp.dot(y_s[...], wout_ref[...], preferred_element_type=F32)

    hn = _rms_norm(h, g2_ref[...]).astype(BF16)
    x = h + 0.5 * _swiglu(hn, wg_ref, wu_ref, wd_ref)
    if final_norm:
        x = _rms_norm(x, gf_ref[...])
    o_ref[...] = x


def _compiler_params(grid_rank):
    return pltpu.CompilerParams(dimension_semantics=("arbitrary",) * grid_rank,
                                vmem_limit_bytes=VMEM_LIMIT_BYTES)


def _resident(shape, layer=None):
    nd = len(shape)
    if layer is None:
        return pl.BlockSpec(shape, lambda *_: (0,) * nd, pipeline_mode=pl.Buffered(1))
    return pl.BlockSpec((None,) + shape, lambda *_: (layer,) + (0,) * nd,
                        pipeline_mode=pl.Buffered(1))


def _ffn_in(x, l, g1, wg, wu, wd, g2, win, sgun, convw, convb):
    tokens, d = x.shape
    hidden = wg.shape[-1]
    zw = win.shape[-1]
    gw = GROUP_WIDTH
    assert zw == 4 * gw
    row = lambda i: (i, 0)
    group_spec = pl.BlockSpec((FFN_TILE, gw), row)
    group_f32 = jax.ShapeDtypeStruct((tokens, gw), F32)
    return pl.pallas_call(
        _ffn_in_kernel,
        grid=(tokens // FFN_TILE,),
        in_specs=[pl.BlockSpec((FFN_TILE, d), row),
                  _resident((1, d), l), _resident((d, hidden), l), _resident((d, hidden), l),
                  _resident((hidden, d), l), _resident((1, d), l), _resident((d, zw), l),
                  _resident((HEADS, HEAD_DIM), l), _resident((CONV_WIDTH, gw), l), _resident((1, gw), l)],
        out_specs=[pl.BlockSpec((FFN_TILE, d), row)] + [group_spec] * 5,
        out_shape=[jax.ShapeDtypeStruct((tokens, d), F32), group_f32,
                   jax.ShapeDtypeStruct((tokens, gw), BF16), group_f32, group_f32, group_f32],
        compiler_params=_compiler_params(1),
        name=f"ffn_in_{l}",
    )(x, g1, wg, wu, wd, g2, win, sgun, convw, convb)


def _state_scratch():
    return [pltpu.VMEM((HEADS, STATE_ROWS, HEAD_DIM), F32),
            pltpu.VMEM((8, 128), F32)]


def _tile_row(n_tiles, reverse):
    return lambda b, j: b * n_tiles + ((n_tiles - 1 - j) if reverse else j)


def _shared_specs(tile_row):
    gw = GROUP_WIDTH
    chunks = MIX_TILE // CHUNK
    rows2 = lambda b, j: (tile_row(b, j), 0)
    rows3 = lambda b, j: (tile_row(b, j), 0, 0)
    return [pl.BlockSpec((2 * HALO_ROWS, gw), rows2),
            pl.BlockSpec((chunks, gw, CHUNK), rows3),
            pl.BlockSpec((MIX_TILE, gw), rows2),
            pl.BlockSpec((chunks, gw, CHUNK), rows3),
            pl.BlockSpec((MIX_TILE, GATE_LANES), rows2),
            pl.BlockSpec((chunks * GATE_ROWS, CHUNK), rows2)]


def _mix_bwd(xm, xcr, l, batch, mw):
    tokens = xm.shape[0]
    n_tiles = tokens // batch // MIX_TILE
    tile_row = _tile_row(n_tiles, True)
    gw = GROUP_WIDTH
    halo_per_tile = MIX_TILE // HALO_ROWS
    last_halo = tokens // HALO_ROWS - 1
    n_diag = gw // MXU_WIDTH
    chunks = MIX_TILE // CHUNK
    shared_shapes = [jax.ShapeDtypeStruct((tokens // MIX_TILE * 2 * HALO_ROWS, gw), F32),
                     jax.ShapeDtypeStruct((tokens // CHUNK, gw, CHUNK), BF16),
                     jax.ShapeDtypeStruct((tokens, gw), BF16),
                     jax.ShapeDtypeStruct((tokens // CHUNK, gw, CHUNK), F32),
                     jax.ShapeDtypeStruct((tokens, GATE_LANES), F32),
                     jax.ShapeDtypeStruct((tokens // CHUNK * GATE_ROWS, CHUNK), F32)]
    return pl.pallas_call(
        functools.partial(_mix_bwd_kernel, n_tiles),
        grid=(batch, n_tiles),
        in_specs=[pl.BlockSpec((MIX_TILE, gw), lambda b, j: (tile_row(b, j), 0)),
                  pl.BlockSpec((HALO_ROWS, gw), lambda b, j: (
                      jnp.maximum(tile_row(b, j) * halo_per_tile - 1, 0), 0)),
                  pl.BlockSpec((HALO_ROWS, gw), lambda b, j: (
                      jnp.minimum((tile_row(b, j) + 1) * halo_per_tile, last_halo), 0)),
                  pl.BlockSpec((MIX_TILE, gw), lambda b, j: (tile_row(b, j), 0)),
                  _resident((CONV_WIDTH, gw), l), _resident((1, gw), l),
                  _resident((n_diag, MXU_WIDTH, MXU_WIDTH), l),
                  _resident((n_diag, MXU_WIDTH, MXU_WIDTH), l),
                  _resident((n_diag, MXU_WIDTH, MXU_WIDTH), l),
                  _resident((n_diag, MXU_WIDTH, MXU_WIDTH), l),
                  _resident((n_diag, MXU_WIDTH, MXU_WIDTH), l),
                  _resident((3 * gw, GATE_LANES), l), _resident((1, GATE_LANES), l)],
        out_specs=[pl.BlockSpec((chunks, gw, CHUNK), lambda b, j: (tile_row(b, j), 0, 0))]
                  + _shared_specs(tile_row),
        out_shape=[jax.ShapeDtypeStruct((tokens // CHUNK, gw, CHUNK), F32)] + shared_shapes,
        scratch_shapes=_state_scratch(),
        compiler_params=_compiler_params(2),
        name=f"mix_bwd_{l}",
    )(xm, xm, xm, xcr, *mw)


def _mix_fwd(ug, vn, sg, xcr, h, hbt, shared, l, batch, sguw, sgub, mhn, skip, wout,
             g2, wg, wu, wd, gf, final_norm):
    tokens, d = h.shape
    hidden = wg.shape[-1]
    n_tiles = tokens // batch // MIX_TILE
    tile_row = _tile_row(n_tiles, False)
    gw = GROUP_WIDTH
    chunks = MIX_TILE // CHUNK
    rows2 = lambda b, j: (tile_row(b, j), 0)
    return pl.pallas_call(
        functools.partial(_mix_fwd_kernel, final_norm),
        grid=(batch, n_tiles),
        in_specs=[pl.BlockSpec((MIX_TILE, gw), rows2),
                  pl.BlockSpec((MIX_TILE, gw), rows2),
                  pl.BlockSpec((MIX_TILE, gw), rows2),
                  pl.BlockSpec((MIX_TILE, gw), rows2),
                  pl.BlockSpec((MIX_TILE, d), rows2),
                  pl.BlockSpec((chunks, gw, CHUNK), lambda b, j: (tile_row(b, j), 0, 0))]
                 + _shared_specs(tile_row)
                 + [_resident((HEADS, CHUNK, CHUNK), l),
                    _resident((HEADS, CHUNK, HEAD_DIM), l), _resident((HEADS, HEAD_DIM, CHUNK), l),
                    _resident((1, gw), l), _resident((2 * gw, d), l),
                    _resident((1, d), l), _resident((d, hidden), l), _resident((d, hidden), l),
                    _resident((hidden, d), l), _resident((1, d))],
        out_specs=pl.BlockSpec((MIX_TILE, d), rows2),
        out_shape=jax.ShapeDtypeStruct((tokens, d), F32),
        scratch_shapes=_state_scratch() + [pltpu.VMEM((chunks, gw, CHUNK), F32),
                                           pltpu.VMEM((MIX_TILE, 2 * gw), BF16)],
        compiler_params=_compiler_params(2),
        name=f"mix_fwd_{l}",
    )(ug, vn, sg, xcr, h, hbt, *shared, sguw, sgub, mhn, skip, wout, g2, wg, wu, wd, gf)


def _diag_blocks(w, transpose=False):
    depth, groups, bi, bo = w.shape
    rows = w.reshape(depth, groups * bi // MXU_WIDTH, MXU_WIDTH, bo)
    idx = jnp.arange(MXU_WIDTH)
    tiled = jnp.einsum('dnro,oc->dnrc', rows, (idx[None, :] % bo == jnp.arange(bo)[:, None]).astype(F32),
                       precision=lax.Precision.HIGHEST)
    dense = jnp.where(idx[:, None] // bi == idx[None, :] // bo, tiled, 0.0)
    if transpose:
        dense = jnp.swapaxes(dense, -1, -2)
    return dense.astype(BF16)


def _pack_gates(w_fwd, b_fwd, w_bwd, b_bwd):
    depth, rows, n = w_fwd.shape
    w = jnp.concatenate([w_fwd, w_bwd, jnp.zeros((depth, rows, GATE_LANES - 2 * n), F32)], axis=-1)
    b = jnp.concatenate([b_fwd, b_bwd, jnp.zeros((depth, GATE_LANES - 2 * n), F32)], axis=-1)
    return w.astype(BF16), b.reshape(depth, 1, GATE_LANES)


def kernel(x, ffn1_norm, ffn1_w_gate, ffn1_w_up, ffn1_w_down, mix_norm, w_in, sgu_norm, sgu_w, sgu_b, conv_w, conv_b, w_q, w_k, w_v, gate_w_fwd, gate_b_fwd, gate_w_bwd, gate_b_bwd, mh_norm, mlstm_skip, w_out, ffn2_norm, ffn2_w_gate, ffn2_w_up, ffn2_w_down, final_norm):
    batch, seq, d = x.shape
    depth = w_in.shape[0]
    assert seq % MIX_TILE == 0 and (batch * seq) % FFN_TILE == 0
    assert sgu_w.shape[1:] == (HEADS, CHUNK, CHUNK) and mh_norm.shape[1:] == (HEADS, HEAD_DIM)
    assert gate_w_fwd.shape[1:] == (3 * GROUP_WIDTH, 2 * HEADS)

    bf = lambda w: w.astype(BF16)
    row3 = lambda g: g.reshape(depth, 1, -1)
    f1 = (row3(ffn1_norm), bf(ffn1_w_gate), bf(ffn1_w_up), bf(ffn1_w_down))
    f2 = (row3(ffn2_norm), bf(ffn2_w_gate), bf(ffn2_w_up), bf(ffn2_w_down))
    wgate, bgate = _pack_gates(gate_w_fwd, gate_b_fwd, gate_w_bwd, gate_b_bwd)
    convb = row3(conv_b)
    mw = (conv_w, convb, _diag_blocks(w_q), _diag_blocks(w_k), _diag_blocks(w_v),
          _diag_blocks(w_q, transpose=True), _diag_blocks(w_v, transpose=True), wgate, bgate)
    sgub = jnp.broadcast_to(sgu_b[..., None], sgu_b.shape + (HEAD_DIM,))
    mhn = jnp.broadcast_to(mh_norm[..., None], mh_norm.shape + (CHUNK,))
    gfin = final_norm.reshape(1, d)

    xt = x.reshape(batch * seq, d)
    for l in range(depth):
        h, ug, vn, xm, xcr, sg = _ffn_in(xt, l, *f1, row3(mix_norm), bf(w_in), sgu_norm, conv_w, convb)
        hbt, *shared = _mix_bwd(xm, xcr, l, batch, mw)
        xt = _mix_fwd(ug, vn, sg, xcr, h, hbt, shared, l, batch, bf(sgu_w), sgub, mhn,
                      row3(mlstm_skip), bf(w_out), *f2, gfin, final_norm=(l == depth - 1))
    return xt.reshape(batch, seq, d)
```

```python
import functools

import jax
import jax.numpy as jnp
from jax import lax
from jax.experimental import pallas as pl
from jax.experimental.pallas import tpu as pltpu

F32 = jnp.float32
BF16 = jnp.bfloat16

EPS = 1e-6
HEADS = 4
HEAD_DIM = 128
CHUNK = 128
GROUP_WIDTH = HEADS * HEAD_DIM
CONV_WIDTH = 5
QKV_BLOCK = 4
MXU_WIDTH = 256
GATE_LANES = 128
GATE_ROWS = 4 * HEADS
IG_LANE = {False: 0, True: 2 * HEADS}
B_LANE = {False: HEADS, True: 3 * HEADS}
STATE_ROWS = HEAD_DIM + 16
HALO_ROWS = 8

FFN_TILE = 512
MIX_TILE = 512
FFN_HIDDEN_CHUNK = 256
VMEM_LIMIT_BYTES = 56 * 1024 * 1024

_NT = (((1,), (1,)), ((), ()))


def _rms_norm(x, g):
    return x * lax.rsqrt(jnp.mean(x * x, axis=-1, keepdims=True) + EPS) * g


def _layer_norm(x, g):
    mu = jnp.mean(x, axis=-1, keepdims=True)
    xc = x - mu
    var = jnp.mean(xc * xc, axis=-1, keepdims=True)
    return xc * lax.rsqrt(var + EPS) * g


def _gelu_tanh(x):
    c2 = 2.0 * (2.0 / jnp.pi) ** 0.5 * 1.4426950408889634
    return x / (1.0 + jnp.exp2(x * (x * x * (-0.044715 * c2) - c2)))


def _log_sigmoid(x):
    return jnp.minimum(x, 0.0) - jnp.log1p(jnp.exp(-jnp.abs(x)))


def _swiglu(xn, wg_ref, wu_ref, wd_ref):
    hidden = wg_ref.shape[1]
    acc = None
    for lo in range(0, hidden, FFN_HIDDEN_CHUNK):
        hi = min(lo + FFN_HIDDEN_CHUNK, hidden)
        gate = jnp.dot(xn, wg_ref[:, lo:hi], preferred_element_type=F32)
        up = jnp.dot(xn, wu_ref[:, lo:hi], preferred_element_type=F32)
        act = (jax.nn.silu(gate) * up).astype(BF16)
        part = jnp.dot(act, wd_ref[lo:hi, :], preferred_element_type=F32)
        acc = part if acc is None else acc + part
    return acc


def _short_conv(rows, convw_ref, convb_ref):
    n = rows.shape[0]
    conv = None
    for tap in range(CONV_WIDTH):
        off = tap - CONV_WIDTH // 2
        shifted = rows if off == 0 else pltpu.roll(rows, (n - off) % n, axis=0)
        term = shifted * convw_ref[tap:tap + 1, :]
        conv = term if conv is None else conv + term
    return jax.nn.silu(conv + convb_ref[...])


def _ffn_in_kernel(x_ref, g1_ref, wg_ref, wu_ref, wd_ref, g2_ref, win_ref, sgun_ref, convw_ref, convb_ref,
                   h_ref, ug_ref, vn_ref, xm_ref, xcr_ref, sg_ref):
    x = x_ref[...]
    xn = _rms_norm(x, g1_ref[...]).astype(BF16)
    h = x + 0.5 * _swiglu(xn, wg_ref, wu_ref, wd_ref)
    h_ref[...] = h
    hn = _rms_norm(h, g2_ref[...]).astype(BF16)
    z = jnp.dot(hn, win_ref[...], preferred_element_type=F32)
    gw = GROUP_WIDTH
    ug_ref[...] = _gelu_tanh(z[:, :gw])
    for hd in range(HEADS):
        cols = slice(hd * HEAD_DIM, (hd + 1) * HEAD_DIM)
        v = z[:, gw + hd * HEAD_DIM:gw + (hd + 1) * HEAD_DIM]
        vn_ref[:, cols] = _layer_norm(_gelu_tanh(v), sgun_ref[hd:hd + 1, :]).astype(BF16)
    xm = z[:, 2 * gw:3 * gw]
    xm_ref[...] = xm
    xcr_ref[...] = _short_conv(xm, convw_ref, convb_ref)
    sg_ref[...] = jax.nn.sigmoid(z[:, 3 * gw:])


def _block_diag_dot(x, bd_ref):
    parts = [jnp.dot(x[:, i * MXU_WIDTH:(i + 1) * MXU_WIDTH], bd_ref[i], preferred_element_type=F32)
             for i in range(bd_ref.shape[0])]
    return jnp.concatenate(parts, axis=1)


def _mlstm_prepare(first, last, xm_ref, xprev_ref, xnext_ref, xcr_ref, convw_ref, convb_ref,
                   bdq_ref, bdk_ref, bdv_ref, bdqt_ref, bdvt_ref, wgate_ref, bgate_ref,
                   xce_ref, qt_ref, k_ref, vt_ref, ga_ref, gr_ref):
    tile = xm_ref.shape[0]
    xm = xm_ref[...]
    prev = jnp.where(first, 0.0, xprev_ref[...])
    nxt = jnp.where(last, 0.0, xnext_ref[...])
    r = HALO_ROWS

    def edge_conv(above, rows, below):
        ext = jnp.concatenate([above, rows, below], axis=0)
        return _short_conv(ext, convw_ref, convb_ref)[r:2 * r]

    top = edge_conv(prev, xm[0:r], xm[r:2 * r])
    bottom = edge_conv(xm[tile - 2 * r:tile - r], xm[tile - r:], nxt)
    xce_ref[0:r, :] = top
    xce_ref[r:, :] = bottom
    xc = jnp.concatenate([top, xcr_ref[r:tile - r, :], bottom], axis=0)

    xc_b = xc.astype(BF16)
    xm_b = xm.astype(BF16)
    q = _block_diag_dot(xc_b, bdq_ref).astype(BF16)
    k = (_block_diag_dot(xc_b, bdk_ref) * (HEAD_DIM ** -0.5)).astype(BF16)
    v = _block_diag_dot(xm_b, bdv_ref).astype(BF16)
    k_ref[...] = k
    gates = (jnp.dot(q, wgate_ref[0:GROUP_WIDTH, :], preferred_element_type=F32)
             + jnp.dot(k, wgate_ref[GROUP_WIDTH:2 * GROUP_WIDTH, :], preferred_element_type=F32)
             + jnp.dot(v, wgate_ref[2 * GROUP_WIDTH:, :], preferred_element_type=F32)
             + bgate_ref[...])
    n_chunks = tile // CHUNK

    for i in range(bdvt_ref.shape[0]):
        blk = slice(i * MXU_WIDTH, (i + 1) * MXU_WIDTH)
        qt = lax.dot_general(bdqt_ref[i], xc_b[:, blk], _NT, preferred_element_type=F32).astype(BF16)
        vt = lax.dot_general(bdvt_ref[i], xm_b[:, blk], _NT, preferred_element_type=F32)
        for c in range(n_chunks):
            qt_ref[c, blk, :] = qt[:, c * CHUNK:(c + 1) * CHUNK]
            vt_ref[c, blk, :] = vt[:, c * CHUNK:(c + 1) * CHUNK]

    lane = lax.broadcasted_iota(jnp.int32, gates.shape, 1)
    is_forget = (((lane >= B_LANE[False]) & (lane < B_LANE[False] + HEADS))
                 | ((lane >= B_LANE[True]) & (lane < B_LANE[True] + HEADS)))
    mixed = jnp.where(is_forget, _log_sigmoid(gates), gates)
    gt = jnp.concatenate([mixed[c * CHUNK:(c + 1) * CHUNK].T[:GATE_ROWS] for c in range(n_chunks)], axis=0)
    n_rows = n_chunks * GATE_ROWS
    hi = gt.astype(BF16)
    rem = gt - hi.astype(F32)
    mid = rem.astype(BF16)
    pieces = jnp.concatenate([hi, mid, (rem - mid.astype(F32)).astype(BF16)], axis=0)
    src = lax.broadcasted_iota(jnp.int32, (CHUNK, CHUNK), 0)
    dst = lax.broadcasted_iota(jnp.int32, (CHUNK, CHUNK), 1)

    def prefix_sum(includes):
        parts = jnp.dot(pieces, includes.astype(BF16), preferred_element_type=F32)
        return parts[:n_rows] + parts[n_rows:2 * n_rows] + parts[2 * n_rows:]

    row = lax.broadcasted_iota(jnp.int32, gt.shape, 0) % GATE_ROWS
    gr = jnp.where((row >= B_LANE[False]) & (row < B_LANE[False] + HEADS), prefix_sum(src <= dst),
                   jnp.where((row >= B_LANE[True]) & (row < B_LANE[True] + HEADS), prefix_sum(src >= dst), gt))
    gr_ref[...] = gr
    a_rows = gr - pltpu.roll(gr, n_rows - HEADS, axis=0)
    pad = jnp.zeros((GATE_LANES - GATE_ROWS, CHUNK), F32)
    for c in range(n_chunks):
        a_c = jnp.concatenate([a_rows[c * GATE_ROWS:(c + 1) * GATE_ROWS], pad], axis=0)
        ga_ref[c * CHUNK:(c + 1) * CHUNK, :] = a_c.T


def _mlstm_direction(reverse, n_chunks, qt_ref, k_ref, vt_ref, ga_ref, gr_ref, state_s, m_s, out_ref):
    key = lax.broadcasted_iota(jnp.int32, (CHUNK, CHUNK), 0)
    qry = lax.broadcasted_iota(jnp.int32, (CHUNK, CHUNK), 1)
    visible = (key >= qry) if reverse else (key <= qry)
    first_row = lax.broadcasted_iota(jnp.int32, (STATE_ROWS - HEAD_DIM, CHUNK), 0) == 0
    edge = 0 if reverse else CHUNK - 1
    ig0, b0 = IG_LANE[reverse], B_LANE[reverse]

    order = [(n_chunks - 1 - i) if reverse else i for i in range(n_chunks)]
    head_cols = [slice(h * HEAD_DIM, (h + 1) * HEAD_DIM) for h in range(HEADS)]
    chunk_rows = [slice(c * CHUNK, (c + 1) * CHUNK) for c in range(n_chunks)]
    gr = [gr_ref[c * GATE_ROWS:(c + 1) * GATE_ROWS, :] for c in range(n_chunks)]
    b_rows = {(c, h): gr[c][b0 + h:b0 + h + 1, :] for c in order for h in range(HEADS)}

    incr = {}
    for c in order:
        for h in range(HEADS):
            b_row = b_rows[c, h]
            ig_row = gr[c][ig0 + h:ig0 + h + 1, :]
            b_edge = jnp.broadcast_to(b_row[:, edge:edge + 1], b_row.shape)
            dec = (b_edge - b_row) + ig_row
            dec_max = jnp.broadcast_to(jnp.max(dec, axis=1, keepdims=True), dec.shape)
            wk = jnp.exp(dec - dec_max)
            lhs = jnp.concatenate([vt_ref[c, head_cols[h], :] * wk, jnp.where(first_row, wk, 0.0)],
                                  axis=0).astype(BF16)
            update = jnp.dot(lhs, k_ref[chunk_rows[c], head_cols[h]], preferred_element_type=F32)
            incr[c, h] = (b_edge, dec_max, update)

    states = [state_s[h] for h in range(HEADS)]
    ms = [m_s[h:h + 1, :] for h in range(HEADS)]
    partial = {}
    for c in order:
        ga = ga_ref[chunk_rows[c], :]
        for h in range(HEADS):
            b_row = b_rows[c, h]
            a_col = ga[:, ig0 + h:ig0 + h + 1]
            state, m_prev = states[h], ms[h]
            d = jnp.where(visible, b_row + a_col, -jnp.inf)
            m_loc = jnp.max(d, axis=0, keepdims=True)
            lhs = jnp.concatenate([k_ref[chunk_rows[c], head_cols[h]], state.astype(BF16)], axis=0)
            both = jnp.dot(lhs, qt_ref[c, head_cols[h], :], preferred_element_type=F32)
            st = both[:CHUNK] * jnp.exp(d - m_loc)
            carried = both[CHUNK:]
            inter = b_row + m_prev
            m_j = jnp.maximum(inter, m_loc)
            w_inter = jnp.exp(inter - m_j)
            w_loc = jnp.exp(m_loc - m_j)
            nq = (w_inter * carried[HEAD_DIM:HEAD_DIM + 1]
                  + w_loc * jnp.sum(st, axis=0, keepdims=True))
            inv = 1.0 / jnp.maximum(jnp.abs(nq), jnp.exp(-m_j))
            partial[c, h] = (st.astype(BF16), w_inter * inv, w_loc * inv, carried[:HEAD_DIM])

            b_edge, dec_max, update = incr[c, h]
            m_new = jnp.maximum(b_edge + m_prev, dec_max)
            states[h] = (jnp.exp(b_edge + m_prev - m_new) * state
                         + jnp.exp(dec_max - m_new) * update)
            ms[h] = m_new
    for h in range(HEADS):
        state_s[h] = states[h]
        m_s[h:h + 1, :] = ms[h]

    for c in order:
        for h in range(HEADS):
            st_b, s_inter, s_loc, carried = partial[c, h]
            num_loc = jnp.dot(vt_ref[c, head_cols[h], :].astype(BF16), st_b, preferred_element_type=F32)
            out_ref[c, head_cols[h], :] = s_inter * carried + s_loc * num_loc


def _reset_state(state_s, m_s):
    state_s[...] = jnp.zeros_like(state_s)
    m_s[...] = jnp.zeros_like(m_s)


def _mix_bwd_kernel(n_tiles, xm_ref, xprev_ref, xnext_ref, xcr_ref, convw_ref, convb_ref, bdq_ref, bdk_ref,
                    bdv_ref, bdqt_ref, bdvt_ref, wgate_ref, bgate_ref,
                    hbt_ref, xce_ref, qt_ref, k_ref, vt_ref, ga_ref, gr_ref, state_s, m_s):
    j = pl.program_id(1)
    seq_tile = n_tiles - 1 - j

    @pl.when(j == 0)
    def _():
        _reset_state(state_s, m_s)

    _mlstm_prepare(seq_tile == 0, seq_tile == n_tiles - 1, xm_ref, xprev_ref, xnext_ref, xcr_ref,
                   convw_ref, convb_ref, bdq_ref, bdk_ref, bdv_ref, bdqt_ref, bdvt_ref, wgate_ref,
                   bgate_ref, xce_ref, qt_ref, k_ref, vt_ref, ga_ref, gr_ref)
    _mlstm_direction(True, xm_ref.shape[0] // CHUNK, qt_ref, k_ref, vt_ref, ga_ref, gr_ref,
                     state_s, m_s, hbt_ref)


def _mix_fwd_kernel(final_norm, ug_ref, vn_ref, sg_ref, xcr_ref, h_ref, hbt_ref, xce_ref, qt_ref, k_ref, vt_ref,
                    ga_ref, gr_ref, sguw_ref, sgub_ref, mhn_ref, skip_ref, wout_ref,
                    g2_ref, wg_ref, wu_ref, wd_ref, gf_ref, o_ref,
                    state_s, m_s, hft_s, y_s):
    tile = h_ref.shape[0]

    @pl.when(pl.program_id(1) == 0)
    def _():
        _reset_state(state_s, m_s)

    _mlstm_direction(False, tile // CHUNK, qt_ref, k_ref, vt_ref, ga_ref, gr_ref, state_s, m_s, hft_s)

    for h in range(HEADS):
        cols = slice(h * HEAD_DIM, (h + 1) * HEAD_DIM)
        for c in range(tile // CHUNK):
            rows = slice(c * CHUNK, (c + 1) * CHUNK)
            mixed = jnp.dot(sguw_ref[h], vn_ref[rows, cols], preferred_element_type=F32) + sgub_ref[h]
            y_s[rows, cols] = (ug_ref[rows, cols] * mixed).astype(BF16)

    n_chunks = tile // CHUNK
    for c in range(n_chunks):
        rows = slice(c * CHUNK, (c + 1) * CHUNK)
        for h in range(HEADS):
            cols = slice(h * HEAD_DIM, (h + 1) * HEAD_DIM)
            xc = xcr_ref[rows, cols]
            if c == 0:
                xc = jnp.concatenate([xce_ref[0:HALO_ROWS, cols], xc[HALO_ROWS:]], axis=0)
            if c == n_chunks - 1:
                xc = jnp.concatenate([xc[:CHUNK - HALO_ROWS], xce_ref[HALO_ROWS:, cols]], axis=0)
            ht = hft_s[c, cols, :] + hbt_ref[c, cols, :]
            mu = jnp.mean(ht, axis=0, keepdims=True)
            hc = ht - mu
            var = jnp.mean(hc * hc, axis=0, keepdims=True)
            hn = (hc * lax.rsqrt(var + EPS) * mhn_ref[h]).T
            ym = (hn + skip_ref[:, cols] * xc) * sg_ref[rows, cols]
            y_s[rows, GROUP_WIDTH + h * HEAD_DIM:GROUP_WIDTH + (h + 1) * HEAD_DIM] = ym.astype(BF16)

    h = h_ref[...] + jnp.dot(y_s[...], wout_ref[...], preferred_element_type=F32)

    hn = _rms_norm(h, g2_ref[...]).astype(BF16)
    x = h + 0.5 * _swiglu(hn, wg_ref, wu_ref, wd_ref)
    if final_norm:
        x = _rms_norm(x, gf_ref[...])
    o_ref[...] = x


def _compiler_params(grid_rank):
    return pltpu.CompilerParams(dimension_semantics=("arbitrary",) * grid_rank,
                                vmem_limit_bytes=VMEM_LIMIT_BYTES)


def _resident(shape, layer=None):
    nd = len(shape)
    if layer is None:
        return pl.BlockSpec(shape, lambda *_: (0,) * nd, pipeline_mode=pl.Buffered(1))
    return pl.BlockSpec((None,) + shape, lambda *_: (layer,) + (0,) * nd,
                        pipeline_mode=pl.Buffered(1))


def _ffn_in(x, l, g1, wg, wu, wd, g2, win, sgun, convw, convb):
    tokens, d = x.shape
    hidden = wg.shape[-1]
    zw = win.shape[-1]
    gw = GROUP_WIDTH
    assert zw == 4 * gw
    row = lambda i: (i, 0)
    group_spec = pl.BlockSpec((FFN_TILE, gw), row)
    group_f32 = jax.ShapeDtypeStruct((tokens, gw), F32)
    return pl.pallas_call(
        _ffn_in_kernel,
        grid=(tokens // FFN_TILE,),
        in_specs=[pl.BlockSpec((FFN_TILE, d), row),
                  _resident((1, d), l), _resident((d, hidden), l), _resident((d, hidden), l),
                  _resident((hidden, d), l), _resident((1, d), l), _resident((d, zw), l),
                  _resident((HEADS, HEAD_DIM), l), _resident((CONV_WIDTH, gw), l), _resident((1, gw), l)],
        out_specs=[pl.BlockSpec((FFN_TILE, d), row)] + [group_spec] * 5,
        out_shape=[jax.ShapeDtypeStruct((tokens, d), F32), group_f32,
                   jax.ShapeDtypeStruct((tokens, gw), BF16), group_f32, group_f32, group_f32],
        compiler_params=_compiler_params(1),
        name=f"ffn_in_{l}",
    )(x, g1, wg, wu, wd, g2, win, sgun, convw, convb)


def _state_scratch():
    return [pltpu.VMEM((HEADS, STATE_ROWS, HEAD_DIM), F32),
            pltpu.VMEM((8, 128), F32)]


def _tile_row(n_tiles, reverse):
    return lambda b, j: b * n_tiles + ((n_tiles - 1 - j) if reverse else j)


def _shared_specs(tile_row):
    gw = GROUP_WIDTH
    chunks = MIX_TILE // CHUNK
    rows2 = lambda b, j: (tile_row(b, j), 0)
    rows3 = lambda b, j: (tile_row(b, j), 0, 0)
    return [pl.BlockSpec((2 * HALO_ROWS, gw), rows2),
            pl.BlockSpec((chunks, gw, CHUNK), rows3),
            pl.BlockSpec((MIX_TILE, gw), rows2),
            pl.BlockSpec((chunks, gw, CHUNK), rows3),
            pl.BlockSpec((MIX_TILE, GATE_LANES), rows2),
            pl.BlockSpec((chunks * GATE_ROWS, CHUNK), rows2)]


def _mix_bwd(xm, xcr, l, batch, mw):
    tokens = xm.shape[0]
    n_tiles = tokens // batch // MIX_TILE
    tile_row = _tile_row(n_tiles, True)
    gw = GROUP_WIDTH
    halo_per_tile = MIX_TILE // HALO_ROWS
    last_halo = tokens // HALO_ROWS - 1
    n_diag = gw // MXU_WIDTH
    chunks = MIX_TILE // CHUNK
    shared_shapes = [jax.ShapeDtypeStruct((tokens // MIX_TILE * 2 * HALO_ROWS, gw), F32),
                     jax.ShapeDtypeStruct((tokens // CHUNK, gw, CHUNK), BF16),
                     jax.ShapeDtypeStruct((tokens, gw), BF16),
                     jax.ShapeDtypeStruct((tokens // CHUNK, gw, CHUNK), F32),
                     jax.ShapeDtypeStruct((tokens, GATE_LANES), F32),
                     jax.ShapeDtypeStruct((tokens // CHUNK * GATE_ROWS, CHUNK), F32)]
    return pl.pallas_call(
        functools.partial(_mix_bwd_kernel, n_tiles),
        grid=(batch, n_tiles),
        in_specs=[pl.BlockSpec((MIX_TILE, gw), lambda b, j: (tile_row(b, j), 0)),
                  pl.BlockSpec((HALO_ROWS, gw), lambda b, j: (
                      jnp.maximum(tile_row(b, j) * halo_per_tile - 1, 0), 0)),
                  pl.BlockSpec((HALO_ROWS, gw), lambda b, j: (
                      jnp.minimum((tile_row(b, j) + 1) * halo_per_tile, last_halo), 0)),
                  pl.BlockSpec((MIX_TILE, gw), lambda b, j: (tile_row(b, j), 0)),
                  _resident((CONV_WIDTH, gw), l), _resident((1, gw), l),
                  _resident((n_diag, MXU_WIDTH, MXU_WIDTH), l),
                  _resident((n_diag, MXU_WIDTH, MXU_WIDTH), l),
                  _resident((n_diag, MXU_WIDTH, MXU_WIDTH), l),
                  _resident((n_diag, MXU_WIDTH, MXU_WIDTH), l),
                  _resident((n_diag, MXU_WIDTH, MXU_WIDTH), l),
                  _resident((3 * gw, GATE_LANES), l), _resident((1, GATE_LANES), l)],
        out_specs=[pl.BlockSpec((chunks, gw, CHUNK), lambda b, j: (tile_row(b, j), 0, 0))]
                  + _shared_specs(tile_row),
        out_shape=[jax.ShapeDtypeStruct((tokens // CHUNK, gw, CHUNK), F32)] + shared_shapes,
        scratch_shapes=_state_scratch(),
        compiler_params=_compiler_params(2),
        name=f"mix_bwd_{l}",
    )(xm, xm, xm, xcr, *mw)


def _mix_fwd(ug, vn, sg, xcr, h, hbt, shared, l, batch, sguw, sgub, mhn, skip, wout,
             g2, wg, wu, wd, gf, final_norm):
    tokens, d = h.shape
    hidden = wg.shape[-1]
    n_tiles = tokens // batch // MIX_TILE
    tile_row = _tile_row(n_tiles, False)
    gw = GROUP_WIDTH
    chunks = MIX_TILE // CHUNK
    rows2 = lambda b, j: (tile_row(b, j), 0)
    return pl.pallas_call(
        functools.partial(_mix_fwd_kernel, final_norm),
        grid=(batch, n_tiles),
        in_specs=[pl.BlockSpec((MIX_TILE, gw), rows2),
                  pl.BlockSpec((MIX_TILE, gw), rows2),
                  pl.BlockSpec((MIX_TILE, gw), rows2),
                  pl.BlockSpec((MIX_TILE, gw), rows2),
                  pl.BlockSpec((MIX_TILE, d), rows2),
                  pl.BlockSpec((chunks, gw, CHUNK), lambda b, j: (tile_row(b, j), 0, 0))]
                 + _shared_specs(tile_row)
                 + [_resident((HEADS, CHUNK, CHUNK), l),
                    _resident((HEADS, CHUNK, HEAD_DIM), l), _resident((HEADS, HEAD_DIM, CHUNK), l),
                    _resident((1, gw), l), _resident((2 * gw, d), l),
                    _resident((1, d), l), _resident((d, hidden), l), _resident((d, hidden), l),
                    _resident((hidden, d), l), _resident((1, d))],
        out_specs=pl.BlockSpec((MIX_TILE, d), rows2),
        out_shape=jax.ShapeDtypeStruct((tokens, d), F32),
        scratch_shapes=_state_scratch() + [pltpu.VMEM((chunks, gw, CHUNK), F32),
                                           pltpu.VMEM((MIX_TILE, 2 * gw), BF16)],
        compiler_params=_compiler_params(2),
        name=f"mix_fwd_{l}",
    )(ug, vn, sg, xcr, h, hbt, *shared, sguw, sgub, mhn, skip, wout, g2, wg, wu, wd, gf)


def _diag_blocks(w, transpose=False):
    depth, groups, bi, bo = w.shape
    rows = w.reshape(depth, groups * bi // MXU_WIDTH, MXU_WIDTH, bo)
    idx = jnp.arange(MXU_WIDTH)
    tiled = jnp.einsum('dnro,oc->dnrc', rows, (idx[None, :] % bo == jnp.arange(bo)[:, None]).astype(F32),
                       precision=lax.Precision.HIGHEST)
    dense = jnp.where(idx[:, None] // bi == idx[None, :] // bo, tiled, 0.0)
    if transpose:
        dense = jnp.swapaxes(dense, -1, -2)
    return dense.astype(BF16)


def _pack_gates(w_fwd, b_fwd, w_bwd, b_bwd):
    depth, rows, n = w_fwd.shape
    w = jnp.concatenate([w_fwd, w_bwd, jnp.zeros((depth, rows, GATE_LANES - 2 * n), F32)], axis=-1)
    b = jnp.concatenate([b_fwd, b_bwd, jnp.zeros((depth, GATE_LANES - 2 * n), F32)], axis=-1)
    return w.astype(BF16), b.reshape(depth, 1, GATE_LANES)


def kernel(x, ffn1_norm, ffn1_w_gate, ffn1_w_up, ffn1_w_down, mix_norm, w_in, sgu_norm, sgu_w, sgu_b, conv_w, conv_b, w_q, w_k, w_v, gate_w_fwd, gate_b_fwd, gate_w_bwd, gate_b_bwd, mh_norm, mlstm_skip, w_out, ffn2_norm, ffn2_w_gate, ffn2_w_up, ffn2_w_down, final_norm):
    batch, seq, d = x.shape
    depth = w_in.shape[0]
    assert seq % MIX_TILE == 0 and (batch * seq) % FFN_TILE == 0
    assert sgu_w.shape[1:] == (HEADS, CHUNK, CHUNK) and mh_norm.shape[1:] == (HEADS, HEAD_DIM)
    assert gate_w_fwd.shape[1:] == (3 * GROUP_WIDTH, 2 * HEADS)

    bf = lambda w: w.astype(BF16)
    row3 = lambda g: g.reshape(depth, 1, -1)
    f1 = (row3(ffn1_norm), bf(ffn1_w_gate), bf(ffn1_w_up), bf(ffn1_w_down))
    f2 = (row3(ffn2_norm), bf(ffn2_w_gate), bf(ffn2_w_up), bf(ffn2_w_down))
    wgate, bgate = _pack_gates(gate_w_fwd, gate_b_fwd, gate_w_bwd, gate_b_bwd)
    convb = row3(conv_b)
    mw = (conv_w, convb, _diag_blocks(w_q), _diag_blocks(w_k), _diag_blocks(w_v),
          _diag_blocks(w_q, transpose=True), _diag_blocks(w_v, transpose=True), wgate, bgate)
    sgub = jnp.broadcast_to(sgu_b[..., None], sgu_b.shape + (HEAD_DIM,))
    mhn = jnp.broadcast_to(mh_norm[..., None], mh_norm.shape + (CHUNK,))
    gfin = final_norm.reshape(1, d)

    xt = x.reshape(batch * seq, d)
    for l in range(depth):
        h, ug, vn, xm, xcr, sg = _ffn_in(xt, l, *f1, row3(mix_norm), bf(w_in), sgu_norm, conv_w, convb)
        hbt, *shared = _mix_bwd(xm, xcr, l, batch, mw)
        xt = _mix_fwd(ug, vn, sg, xcr, h, hbt, shared, l, batch, bf(sgu_w), sgub, mhn,
                      row3(mlstm_skip), bf(w_out), *f2, gfin, final_norm=(l == depth - 1))
    return xt.reshape(batch, seq, d)
```

```python
import functools

import jax
import jax.numpy as jnp
from jax import lax
from jax.experimental import pallas as pl
from jax.experimental.pallas import tpu as pltpu

F32 = jnp.float32
BF16 = jnp.bfloat16

EPS = 1e-6
HEADS = 4
HEAD_DIM = 128
CHUNK = 128
GROUP_WIDTH = HEADS * HEAD_DIM
CONV_WIDTH = 5
QKV_BLOCK = 4
MXU_WIDTH = 256
GATE_LANES = 128
GATE_ROWS = 4 * HEADS
IG_LANE = {False: 0, True: 2 * HEADS}
B_LANE = {False: HEADS, True: 3 * HEADS}
STATE_ROWS = HEAD_DIM + 32
HALO_ROWS = 8

FFN_TILE = 512
MIX_TILE = 512
FFN_HIDDEN_CHUNK = 256
VMEM_LIMIT_BYTES = 56 * 1024 * 1024

_NT = (((1,), (1,)), ((), ()))


def _rms_norm(x, g):
    return x * lax.rsqrt(jnp.mean(x * x, axis=-1, keepdims=True) + EPS) * g


def _layer_norm(x, g):
    mu = jnp.mean(x, axis=-1, keepdims=True)
    xc = x - mu
    var = jnp.mean(xc * xc, axis=-1, keepdims=True)
    return xc * lax.rsqrt(var + EPS) * g


def _gelu_tanh(x):
    c2 = 2.0 * (2.0 / jnp.pi) ** 0.5 * 1.4426950408889634
    return x / (1.0 + jnp.exp2(x * (x * x * (-0.044715 * c2) - c2)))


def _log_sigmoid(x):
    return jnp.minimum(x, 0.0) - jnp.log1p(jnp.exp(-jnp.abs(x)))


def _swiglu(xn, wg_ref, wu_ref, wd_ref):
    hidden = wg_ref.shape[1]
    acc = None
    for lo in range(0, hidden, FFN_HIDDEN_CHUNK):
        hi = min(lo + FFN_HIDDEN_CHUNK, hidden)
        gate = jnp.dot(xn, wg_ref[:, lo:hi], preferred_element_type=F32)
        up = jnp.dot(xn, wu_ref[:, lo:hi], preferred_element_type=F32)
        act = (jax.nn.silu(gate) * up).astype(BF16)
        part = jnp.dot(act, wd_ref[lo:hi, :], preferred_element_type=F32)
        acc = part if acc is None else acc + part
    return acc


def _short_conv(rows, convw_ref, convb_ref):
    n = rows.shape[0]
    conv = None
    for tap in range(CONV_WIDTH):
        off = tap - CONV_WIDTH // 2
        shifted = rows if off == 0 else pltpu.roll(rows, (n - off) % n, axis=0)
        term = shifted * convw_ref[tap:tap + 1, :]
        conv = term if conv is None else conv + term
    return jax.nn.silu(conv + convb_ref[...])


def _ffn_in_kernel(x_ref, g1_ref, wg_ref, wu_ref, wd_ref, g2_ref, win_ref, sgun_ref, convw_ref, convb_ref,
                   h_ref, ug_ref, vn_ref, xm_ref, xcr_ref, sg_ref):
    x = x_ref[...]
    xn = _rms_norm(x, g1_ref[...]).astype(BF16)
    h = x + 0.5 * _swiglu(xn, wg_ref, wu_ref, wd_ref)
    h_ref[...] = h
    hn = _rms_norm(h, g2_ref[...]).astype(BF16)
    z = jnp.dot(hn, win_ref[...], preferred_element_type=F32)
    gw = GROUP_WIDTH
    ug_ref[...] = _gelu_tanh(z[:, :gw])
    for hd in range(HEADS):
        cols = slice(hd * HEAD_DIM, (hd + 1) * HEAD_DIM)
        v = z[:, gw + hd * HEAD_DIM:gw + (hd + 1) * HEAD_DIM]
        vn_ref[:, cols] = _layer_norm(_gelu_tanh(v), sgun_ref[hd:hd + 1, :]).astype(BF16)
    xm = z[:, 2 * gw:3 * gw]
    xm_ref[...] = xm
    xcr_ref[...] = _short_conv(xm, convw_ref, convb_ref)
    sg_ref[...] = jax.nn.sigmoid(z[:, 3 * gw:])


def _block_diag_dot(x, bd_ref):
    parts = [jnp.dot(x[:, i * MXU_WIDTH:(i + 1) * MXU_WIDTH], bd_ref[i], preferred_element_type=F32)
             for i in range(bd_ref.shape[0])]
    return jnp.concatenate(parts, axis=1)


def _mlstm_prepare(first, last, xm_ref, xprev_ref, xnext_ref, xcr_ref, convw_ref, convb_ref,
                   bdq_ref, bdk_ref, bdv_ref, bdqt_ref, bdvt_ref, wgate_ref, bgate_ref,
                   xce_ref, qt_ref, k_ref, vt_ref, ga_ref, gr_ref):
    tile = xm_ref.shape[0]
    xm = xm_ref[...]
    prev = jnp.where(first, 0.0, xprev_ref[...])
    nxt = jnp.where(last, 0.0, xnext_ref[...])
    r = HALO_ROWS

    def edge_conv(above, rows, below):
        ext = jnp.concatenate([above, rows, below], axis=0)
        return _short_conv(ext, convw_ref, convb_ref)[r:2 * r]

    top = edge_conv(prev, xm[0:r], xm[r:2 * r])
    bottom = edge_conv(xm[tile - 2 * r:tile - r], xm[tile - r:], nxt)
    xce_ref[0:r, :] = top
    xce_ref[r:, :] = bottom
    xc = jnp.concatenate([top, xcr_ref[r:tile - r, :], bottom], axis=0)

    xc_b = xc.astype(BF16)
    xm_b = xm.astype(BF16)
    q = _block_diag_dot(xc_b, bdq_ref).astype(BF16)
    k = (_block_diag_dot(xc_b, bdk_ref) * (HEAD_DIM ** -0.5)).astype(BF16)
    v = _block_diag_dot(xm_b, bdv_ref).astype(BF16)
    k_ref[...] = k
    gates = (jnp.dot(q, wgate_ref[0:GROUP_WIDTH, :], preferred_element_type=F32)
             + jnp.dot(k, wgate_ref[GROUP_WIDTH:2 * GROUP_WIDTH, :], preferred_element_type=F32)
             + jnp.dot(v, wgate_ref[2 * GROUP_WIDTH:, :], preferred_element_type=F32)
             + bgate_ref[...])
    n_chunks = tile // CHUNK

    for i in range(bdvt_ref.shape[0]):
        blk = slice(i * MXU_WIDTH, (i + 1) * MXU_WIDTH)
        qt = lax.dot_general(bdqt_ref[i], xc_b[:, blk], _NT, preferred_element_type=F32).astype(BF16)
        vt = lax.dot_general(bdvt_ref[i], xm_b[:, blk], _NT, preferred_element_type=F32)
        for c in range(n_chunks):
            qt_ref[c, blk, :] = qt[:, c * CHUNK:(c + 1) * CHUNK]
            vt_ref[c, blk, :] = vt[:, c * CHUNK:(c + 1) * CHUNK]

    lane = lax.broadcasted_iota(jnp.int32, gates.shape, 1)
    is_forget = (((lane >= B_LANE[False]) & (lane < B_LANE[False] + HEADS))
                 | ((lane >= B_LANE[True]) & (lane < B_LANE[True] + HEADS)))
    mixed = jnp.where(is_forget, _log_sigmoid(gates), gates)
    gt = jnp.concatenate([mixed[c * CHUNK:(c + 1) * CHUNK].T[:GATE_ROWS] for c in range(n_chunks)], axis=0)
    n_rows = n_chunks * GATE_ROWS
    hi = gt.astype(BF16)
    rem = gt - hi.astype(F32)
    mid = rem.astype(BF16)
    pieces = jnp.concatenate([hi, mid, (rem - mid.astype(F32)).astype(BF16)], axis=0)
    src = lax.broadcasted_iota(jnp.int32, (CHUNK, CHUNK), 0)
    dst = lax.broadcasted_iota(jnp.int32, (CHUNK, CHUNK), 1)

    def prefix_sum(includes):
        parts = jnp.dot(pieces, includes.astype(BF16), preferred_element_type=F32)
        return parts[:n_rows] + parts[n_rows:2 * n_rows] + parts[2 * n_rows:]

    row = lax.broadcasted_iota(jnp.int32, gt.shape, 0) % GATE_ROWS
    gr = jnp.where((row >= B_LANE[False]) & (row < B_LANE[False] + HEADS), prefix_sum(src <= dst),
                   jnp.where((row >= B_LANE[True]) & (row < B_LANE[True] + HEADS), prefix_sum(src >= dst), gt))
    gr_ref[...] = gr
    a_rows = gr - pltpu.roll(gr, n_rows - HEADS, axis=0)
    pad = jnp.zeros((GATE_LANES - GATE_ROWS, CHUNK), F32)
    for c in range(n_chunks):
        a_c = jnp.concatenate([a_rows[c * GATE_ROWS:(c + 1) * GATE_ROWS], pad], axis=0)
        ga_ref[c * CHUNK:(c + 1) * CHUNK, :] = a_c.T


def _mlstm_direction(reverse, n_chunks, qt_ref, k_ref, vt_ref, ga_ref, gr_ref, state_s, m_s, out_ref):
    key = lax.broadcasted_iota(jnp.int32, (CHUNK, CHUNK), 0)
    qry = lax.broadcasted_iota(jnp.int32, (CHUNK, CHUNK), 1)
    visible = (key >= qry) if reverse else (key <= qry)
    first_row = lax.broadcasted_iota(jnp.int32, (STATE_ROWS - HEAD_DIM, CHUNK), 0) == 0
    edge = 0 if reverse else CHUNK - 1
    ig0, b0 = IG_LANE[reverse], B_LANE[reverse]

    order = [(n_chunks - 1 - i) if reverse else i for i in range(n_chunks)]
    head_cols = [slice(h * HEAD_DIM, (h + 1) * HEAD_DIM) for h in range(HEADS)]
    chunk_rows = [slice(c * CHUNK, (c + 1) * CHUNK) for c in range(n_chunks)]
    gr = [gr_ref[c * GATE_ROWS:(c + 1) * GATE_ROWS, :] for c in range(n_chunks)]
    b_rows = {(c, h): gr[c][b0 + h:b0 + h + 1, :] for c in order for h in range(HEADS)}

    incr = {}
    for c in order:
        for h in range(HEADS):
            b_row = b_rows[c, h]
            ig_row = gr[c][ig0 + h:ig0 + h + 1, :]
            b_edge = jnp.broadcast_to(b_row[:, edge:edge + 1], b_row.shape)
            dec = (b_edge - b_row) + ig_row
            dec_max = jnp.broadcast_to(jnp.max(dec, axis=1, keepdims=True), dec.shape)
            wk = jnp.exp(dec - dec_max)
            lhs = jnp.concatenate([vt_ref[c, head_cols[h], :] * wk, jnp.where(first_row, wk, 0.0)],
                                  axis=0).astype(BF16)
            update = jnp.dot(lhs, k_ref[chunk_rows[c], head_cols[h]], preferred_element_type=F32)
            incr[c, h] = (b_edge, dec_max, update)

    states = [state_s[h] for h in range(HEADS)]
    ms = [m_s[h:h + 1, :] for h in range(HEADS)]
    partial = {}
    for c in order:
        ga = ga_ref[chunk_rows[c], :]
        for h in range(HEADS):
            b_row = b_rows[c, h]
            a_col = ga[:, ig0 + h:ig0 + h + 1]
            state, m_prev = states[h], ms[h]
            d = jnp.where(visible, b_row + a_col, -jnp.inf)
            m_loc = jnp.max(d, axis=0, keepdims=True)
            lhs = jnp.concatenate([k_ref[chunk_rows[c], head_cols[h]], state.astype(BF16)], axis=0)
            both = jnp.dot(lhs, qt_ref[c, head_cols[h], :], preferred_element_type=F32)
            st = both[:CHUNK] * jnp.exp(d - m_loc)
            carried = both[CHUNK:]
            inter = b_row + m_prev
            m_j = jnp.maximum(inter, m_loc)
            w_inter = jnp.exp(inter - m_j)
            w_loc = jnp.exp(m_loc - m_j)
            nq = (w_inter * carried[HEAD_DIM:HEAD_DIM + 1]
                  + w_loc * jnp.sum(st, axis=0, keepdims=True))
            inv = 1.0 / jnp.maximum(jnp.abs(nq), jnp.exp(-m_j))
            partial[c, h] = (st.astype(BF16), w_inter * inv, w_loc * inv, carried[:HEAD_DIM])

            b_edge, dec_max, update = incr[c, h]
            m_new = jnp.maximum(b_edge + m_prev, dec_max)
            states[h] = (jnp.exp(b_edge + m_prev - m_new) * state
                         + jnp.exp(dec_max - m_new) * update)
            ms[h] = m_new
    for h in range(HEADS):
        state_s[h] = states[h]
        m_s[h:h + 1, :] = ms[h]

    for c in order:
        for h in range(HEADS):
            st_b, s_inter, s_loc, carried = partial[c, h]
            num_loc = jnp.dot(vt_ref[c, head_cols[h], :].astype(BF16), st_b, preferred_element_type=F32)
            out_ref[c, head_cols[h], :] = s_inter * carried + s_loc * num_loc


def _reset_state(state_s, m_s):
    state_s[...] = jnp.zeros_like(state_s)
    m_s[...] = jnp.zeros_like(m_s)


def _mix_bwd_kernel(n_tiles, xm_ref, xprev_ref, xnext_ref, xcr_ref, convw_ref, convb_ref, bdq_ref, bdk_ref,
                    bdv_ref, bdqt_ref, bdvt_ref, wgate_ref, bgate_ref,
                    hbt_ref, xce_ref, qt_ref, k_ref, vt_ref, ga_ref, gr_ref, state_s, m_s):
    j = pl.program_id(1)
    seq_tile = n_tiles - 1 - j

    @pl.when(j == 0)
    def _():
        _reset_state(state_s, m_s)

    _mlstm_prepare(seq_tile == 0, seq_tile == n_tiles - 1, xm_ref, xprev_ref, xnext_ref, xcr_ref,
                   convw_ref, convb_ref, bdq_ref, bdk_ref, bdv_ref, bdqt_ref, bdvt_ref, wgate_ref,
                   bgate_ref, xce_ref, qt_ref, k_ref, vt_ref, ga_ref, gr_ref)
    _mlstm_direction(True, xm_ref.shape[0] // CHUNK, qt_ref, k_ref, vt_ref, ga_ref, gr_ref,
                     state_s, m_s, hbt_ref)


def _mix_fwd_kernel(final_norm, ug_ref, vn_ref, sg_ref, xcr_ref, h_ref, hbt_ref, xce_ref, qt_ref, k_ref, vt_ref,
                    ga_ref, gr_ref, sguw_ref, sgub_ref, mhn_ref, skip_ref, wout_ref,
                    g2_ref, wg_ref, wu_ref, wd_ref, gf_ref, o_ref,
                    state_s, m_s, hft_s, y_s):
    tile = h_ref.shape[0]

    @pl.when(pl.program_id(1) == 0)
    def _():
        _reset_state(state_s, m_s)

    _mlstm_direction(False, tile // CHUNK, qt_ref, k_ref, vt_ref, ga_ref, gr_ref, state_s, m_s, hft_s)

    for h in range(HEADS):
        cols = slice(h * HEAD_DIM, (h + 1) * HEAD_DIM)
        for c in range(tile // CHUNK):
            rows = slice(c * CHUNK, (c + 1) * CHUNK)
            mixed = jnp.dot(sguw_ref[h], vn_ref[rows, cols], preferred_element_type=F32) + sgub_ref[h]
            y_s[rows, cols] = (ug_ref[rows, cols] * mixed).astype(BF16)

    n_chunks = tile // CHUNK
    for c in range(n_chunks):
        rows = slice(c * CHUNK, (c + 1) * CHUNK)
        for h in range(HEADS):
            cols = slice(h * HEAD_DIM, (h + 1) * HEAD_DIM)
            xc = xcr_ref[rows, cols]
            if c == 0:
                xc = jnp.concatenate([xce_ref[0:HALO_ROWS, cols], xc[HALO_ROWS:]], axis=0)
            if c == n_chunks - 1:
                xc = jnp.concatenate([xc[:CHUNK - HALO_ROWS], xce_ref[HALO_ROWS:, cols]], axis=0)
            ht = hft_s[c, cols, :] + hbt_ref[c, cols, :]
            mu = jnp.mean(ht, axis=0, keepdims=True)
            hc = ht - mu
            var = jnp.mean(hc * hc, axis=0, keepdims=True)
            hn = (hc * lax.rsqrt(var + EPS) * mhn_ref[h]).T
            ym = (hn + skip_ref[:, cols] * xc) * sg_ref[rows, cols]
            y_s[rows, GROUP_WIDTH + h * HEAD_DIM:GROUP_WIDTH + (h + 1) * HEAD_DIM] = ym.astype(BF16)

    h = h_ref[...] + jnp.dot(y_s[...], wout_ref[...], preferred_element_type=F32)

    hn = _rms_norm(h, g2_ref[...]).astype(BF16)
    x = h + 0.5 * _swiglu(hn, wg_ref, wu_ref, wd_ref)
    if final_norm:
        x = _rms_norm(x, gf_ref[...])
    o_ref[...] = x


def _compiler_params(grid_rank):
    return pltpu.CompilerParams(dimension_semantics=("arbitrary",) * grid_rank,
                                vmem_limit_bytes=VMEM_LIMIT_BYTES)


def _resident(shape, layer=None):
    nd = len(shape)
    if layer is None:
        return pl.BlockSpec(shape, lambda *_: (0,) * nd, pipeline_mode=pl.Buffered(1))
    return pl.BlockSpec((None,) + shape, lambda *_: (layer,) + (0,) * nd,
                        pipeline_mode=pl.Buffered(1))


def _ffn_in(x, l, g1, wg, wu, wd, g2, win, sgun, convw, convb):
    tokens, d = x.shape
    hidden = wg.shape[-1]
    zw = win.shape[-1]
    gw = GROUP_WIDTH
    assert zw == 4 * gw
    row = lambda i: (i, 0)
    group_spec = pl.BlockSpec((FFN_TILE, gw), row)
    group_f32 = jax.ShapeDtypeStruct((tokens, gw), F32)
    return pl.pallas_call(
        _ffn_in_kernel,
        grid=(tokens // FFN_TILE,),
        in_specs=[pl.BlockSpec((FFN_TILE, d), row),
                  _resident((1, d), l), _resident((d, hidden), l), _resident((d, hidden), l),
                  _resident((hidden, d), l), _resident((1, d), l), _resident((d, zw), l),
                  _resident((HEADS, HEAD_DIM), l), _resident((CONV_WIDTH, gw), l), _resident((1, gw), l)],
        out_specs=[pl.BlockSpec((FFN_TILE, d), row)] + [group_spec] * 5,
        out_shape=[jax.ShapeDtypeStruct((tokens, d), F32), group_f32,
                   jax.ShapeDtypeStruct((tokens, gw), BF16), group_f32, group_f32, group_f32],
        compiler_params=_compiler_params(1),
        name=f"ffn_in_{l}",
    )(x, g1, wg, wu, wd, g2, win, sgun, convw, convb)


def _state_scratch():
    return [pltpu.VMEM((HEADS, STATE_ROWS, HEAD_DIM), F32),
            pltpu.VMEM((8, 128), F32)]


def _tile_row(n_tiles, reverse):
    return lambda b, j: b * n_tiles + ((n_tiles - 1 - j) if reverse else j)


def _shared_specs(tile_row):
    gw = GROUP_WIDTH
    chunks = MIX_TILE // CHUNK
    rows2 = lambda b, j: (tile_row(b, j), 0)
    rows3 = lambda b, j: (tile_row(b, j), 0, 0)
    return [pl.BlockSpec((2 * HALO_ROWS, gw), rows2),
            pl.BlockSpec((chunks, gw, CHUNK), rows3),
            pl.BlockSpec((MIX_TILE, gw), rows2),
            pl.BlockSpec((chunks, gw, CHUNK), rows3),
            pl.BlockSpec((MIX_TILE, GATE_LANES), rows2),
            pl.BlockSpec((chunks * GATE_ROWS, CHUNK), rows2)]


def _mix_bwd(xm, xcr, l, batch, mw):
    tokens = xm.shape[0]
    n_tiles = tokens // batch // MIX_TILE
    tile_row = _tile_row(n_tiles, True)
    gw = GROUP_WIDTH
    halo_per_tile = MIX_TILE // HALO_ROWS
    last_halo = tokens // HALO_ROWS - 1
    n_diag = gw // MXU_WIDTH
    chunks = MIX_TILE // CHUNK
    shared_shapes = [jax.ShapeDtypeStruct((tokens // MIX_TILE * 2 * HALO_ROWS, gw), F32),
                     jax.ShapeDtypeStruct((tokens // CHUNK, gw, CHUNK), BF16),
                     jax.ShapeDtypeStruct((tokens, gw), BF16),
                     jax.ShapeDtypeStruct((tokens // CHUNK, gw, CHUNK), F32),
                     jax.ShapeDtypeStruct((tokens, GATE_LANES), F32),
                     jax.ShapeDtypeStruct((tokens // CHUNK * GATE_ROWS, CHUNK), F32)]
    return pl.pallas_call(
        functools.partial(_mix_bwd_kernel, n_tiles),
        grid=(batch, n_tiles),
        in_specs=[pl.BlockSpec((MIX_TILE, gw), lambda b, j: (tile_row(b, j), 0)),
                  pl.BlockSpec((HALO_ROWS, gw), lambda b, j: (
                      jnp.maximum(tile_row(b, j) * halo_per_tile - 1, 0), 0)),
                  pl.BlockSpec((HALO_ROWS, gw), lambda b, j: (
                      jnp.minimum((tile_row(b, j) + 1) * halo_per_tile, last_halo), 0)),
                  pl.BlockSpec((MIX_TILE, gw), lambda b, j: (tile_row(b, j), 0)),
                  _resident((CONV_WIDTH, gw), l), _resident((1, gw), l),
                  _resident((n_diag, MXU_WIDTH, MXU_WIDTH), l),
                  _resident((n_diag, MXU_WIDTH, MXU_WIDTH), l),
                  _resident((n_diag, MXU_WIDTH, MXU_WIDTH), l),
                  _resident((n_diag, MXU_WIDTH, MXU_WIDTH), l),
                  _resident((n_diag, MXU_WIDTH, MXU_WIDTH), l),
                  _resident((3 * gw, GATE_LANES), l), _resident((1, GATE_LANES), l)],
        out_specs=[pl.BlockSpec((chunks, gw, CHUNK), lambda b, j: (tile_row(b, j), 0, 0))]
                  + _shared_specs(tile_row),
        out_shape=[jax.ShapeDtypeStruct((tokens // CHUNK, gw, CHUNK), F32)] + shared_shapes,
        scratch_shapes=_state_scratch(),
        compiler_params=_compiler_params(2),
        name=f"mix_bwd_{l}",
    )(xm, xm, xm, xcr, *mw)


def _mix_fwd(ug, vn, sg, xcr, h, hbt, shared, l, batch, sguw, sgub, mhn, skip, wout,
             g2, wg, wu, wd, gf, final_norm):
    tokens, d = h.shape
    hidden = wg.shape[-1]
    n_tiles = tokens // batch // MIX_TILE
    tile_row = _tile_row(n_tiles, False)
    gw = GROUP_WIDTH
    chunks = MIX_TILE // CHUNK
    rows2 = lambda b, j: (tile_row(b, j), 0)
    return pl.pallas_call(
        functools.partial(_mix_fwd_kernel, final_norm),
        grid=(batch, n_tiles),
        in_specs=[pl.BlockSpec((MIX_TILE, gw), rows2),
                  pl.BlockSpec((MIX_TILE, gw), rows2),
                  pl.BlockSpec((MIX_TILE, gw), rows2),
                  pl.BlockSpec((MIX_TILE, gw), rows2),
                  pl.BlockSpec((MIX_TILE, d), rows2),
                  pl.BlockSpec((chunks, gw, CHUNK), lambda b, j: (tile_row(b, j), 0, 0))]
                 + _shared_specs(tile_row)
                 + [_resident((HEADS, CHUNK, CHUNK), l),
                    _resident((HEADS, CHUNK, HEAD_DIM), l), _resident((HEADS, HEAD_DIM, CHUNK), l),
                    _resident((1, gw), l), _resident((2 * gw, d), l),
                    _resident((1, d), l), _resident((d, hidden), l), _resident((d, hidden), l),
                    _resident((hidden, d), l), _resident((1, d))],
        out_specs=pl.BlockSpec((MIX_TILE, d), rows2),
        out_shape=jax.ShapeDtypeStruct((tokens, d), F32),
        scratch_shapes=_state_scratch() + [pltpu.VMEM((chunks, gw, CHUNK), F32),
                                           pltpu.VMEM((MIX_TILE, 2 * gw), BF16)],
        compiler_params=_compiler_params(2),
        name=f"mix_fwd_{l}",
    )(ug, vn, sg, xcr, h, hbt, *shared, sguw, sgub, mhn, skip, wout, g2, wg, wu, wd, gf)


def _diag_blocks(w, transpose=False):
    depth, groups, bi, bo = w.shape
    rows = w.reshape(depth, groups * bi // MXU_WIDTH, MXU_WIDTH, bo)
    idx = jnp.arange(MXU_WIDTH)
    tiled = jnp.einsum('dnro,oc->dnrc', rows, (idx[None, :] % bo == jnp.arange(bo)[:, None]).astype(F32),
                       precision=lax.Precision.HIGHEST)
    dense = jnp.where(idx[:, None] // bi == idx[None, :] // bo, tiled, 0.0)
    if transpose:
        dense = jnp.swapaxes(dense, -1, -2)
    return dense.astype(BF16)


def _pack_gates(w_fwd, b_fwd, w_bwd, b_bwd):
    depth, rows, n = w_fwd.shape
    w = jnp.concatenate([w_fwd, w_bwd, jnp.zeros((depth, rows, GATE_LANES - 2 * n), F32)], axis=-1)
    b = jnp.concatenate([b_fwd, b_bwd, jnp.zeros((depth, GATE_LANES - 2 * n), F32)], axis=-1)
    return w.astype(BF16), b.reshape(depth, 1, GATE_LANES)


def kernel(x, ffn1_norm, ffn1_w_gate, ffn1_w_up, ffn1_w_down, mix_norm, w_in, sgu_norm, sgu_w, sgu_b, conv_w, conv_b, w_q, w_k, w_v, gate_w_fwd, gate_b_fwd, gate_w_bwd, gate_b_bwd, mh_norm, mlstm_skip, w_out, ffn2_norm, ffn2_w_gate, ffn2_w_up, ffn2_w_down, final_norm):
    batch, seq, d = x.shape
    depth = w_in.shape[0]
    assert seq % MIX_TILE == 0 and (batch * seq) % FFN_TILE == 0
    assert sgu_w.shape[1:] == (HEADS, CHUNK, CHUNK) and mh_norm.shape[1:] == (HEADS, HEAD_DIM)
    assert gate_w_fwd.shape[1:] == (3 * GROUP_WIDTH, 2 * HEADS)

    bf = lambda w: w.astype(BF16)
    row3 = lambda g: g.reshape(depth, 1, -1)
    f1 = (row3(ffn1_norm), bf(ffn1_w_gate), bf(ffn1_w_up), bf(ffn1_w_down))
    f2 = (row3(ffn2_norm), bf(ffn2_w_gate), bf(ffn2_w_up), bf(ffn2_w_down))
    wgate, bgate = _pack_gates(gate_w_fwd, gate_b_fwd, gate_w_bwd, gate_b_bwd)
    convb = row3(conv_b)
    mw = (conv_w, convb, _diag_blocks(w_q), _diag_blocks(w_k), _diag_blocks(w_v),
          _diag_blocks(w_q, transpose=True), _diag_blocks(w_v, transpose=True), wgate, bgate)
    sgub = jnp.broadcast_to(sgu_b[..., None], sgu_b.shape + (HEAD_DIM,))
    mhn = jnp.broadcast_to(mh_norm[..., None], mh_norm.shape + (CHUNK,))
    gfin = final_norm.reshape(1, d)

    xt = x.reshape(batch * seq, d)
    for l in range(depth):
        h, ug, vn, xm, xcr, sg = _ffn_in(xt, l, *f1, row3(mix_norm), bf(w_in), sgu_norm, conv_w, convb)
        hbt, *shared = _mix_bwd(xm, xcr, l, batch, mw)
        xt = _mix_fwd(ug, vn, sg, xcr, h, hbt, shared, l, batch, bf(sgu_w), sgub, mhn,
                      row3(mlstm_skip), bf(w_out), *f2, gfin, final_norm=(l == depth - 1))
    return xt.reshape(batch, seq, d)
```

```python
import functools

import jax
import jax.numpy as jnp
from jax import lax
from jax.experimental import pallas as pl
from jax.experimental.pallas import tpu as pltpu

F32 = jnp.float32
BF16 = jnp.bfloat16

EPS = 1e-6
HEADS = 4
HEAD_DIM = 128
CHUNK = 128
GROUP_WIDTH = HEADS * HEAD_DIM
CONV_WIDTH = 5
QKV_BLOCK = 4
MXU_WIDTH = 256
GATE_LANES = 128
GATE_ROWS = 4 * HEADS
IG_LANE = {False: 0, True: 2 * HEADS}
B_LANE = {False: HEADS, True: 3 * HEADS}
STATE_ROWS = HEAD_DIM + 32
HALO_ROWS = 8

FFN_TILE = 512
MIX_TILE = 512
FFN_HIDDEN_CHUNK = 256
VMEM_LIMIT_BYTES = 56 * 1024 * 1024

_NT = (((1,), (1,)), ((), ()))


def _rms_norm(x, g):
    return x * lax.rsqrt(jnp.mean(x * x, axis=-1, keepdims=True) + EPS) * g


def _layer_norm(x, g):
    mu = jnp.mean(x, axis=-1, keepdims=True)
    xc = x - mu
    var = jnp.mean(xc * xc, axis=-1, keepdims=True)
    return xc * lax.rsqrt(var + EPS) * g


def _gelu_tanh(x):
    c2 = 2.0 * (2.0 / jnp.pi) ** 0.5 * 1.4426950408889634
    return x / (1.0 + jnp.exp2(x * (x * x * (-0.044715 * c2) - c2)))


def _log_sigmoid(x):
    return jnp.minimum(x, 0.0) - jnp.log1p(jnp.exp(-jnp.abs(x)))


def _swiglu(xn, wg_ref, wu_ref, wd_ref):
    hidden = wg_ref.shape[1]
    acc = None
    for lo in range(0, hidden, FFN_HIDDEN_CHUNK):
        hi = min(lo + FFN_HIDDEN_CHUNK, hidden)
        gate = jnp.dot(xn, wg_ref[:, lo:hi], preferred_element_type=F32)
        up = jnp.dot(xn, wu_ref[:, lo:hi], preferred_element_type=F32)
        act = (jax.nn.silu(gate) * up).astype(BF16)
        part = jnp.dot(act, wd_ref[lo:hi, :], preferred_element_type=F32)
        acc = part if acc is None else acc + part
    return acc


def _short_conv(rows, convw_ref, convb_ref):
    n = rows.shape[0]
    conv = None
    for tap in range(CONV_WIDTH):
        off = tap - CONV_WIDTH // 2
        shifted = rows if off == 0 else pltpu.roll(rows, (n - off) % n, axis=0)
        term = shifted * convw_ref[tap:tap + 1, :]
        conv = term if conv is None else conv + term
    return jax.nn.silu(conv + convb_ref[...])


def _ffn_in_kernel(x_ref, g1_ref, wg_ref, wu_ref, wd_ref, g2_ref, win_ref, sgun_ref, convw_ref, convb_ref,
                   h_ref, ug_ref, vn_ref, xm_ref, xcr_ref, sg_ref):
    x = x_ref[...]
    xn = _rms_norm(x, g1_ref[...]).astype(BF16)
    h = x + 0.5 * _swiglu(xn, wg_ref, wu_ref, wd_ref)
    h_ref[...] = h
    hn = _rms_norm(h, g2_ref[...]).astype(BF16)
    z = jnp.dot(hn, win_ref[...], preferred_element_type=F32)
    gw = GROUP_WIDTH
    ug_ref[...] = _gelu_tanh(z[:, :gw])
    for hd in range(HEADS):
        cols = slice(hd * HEAD_DIM, (hd + 1) * HEAD_DIM)
        v = z[:, gw + hd * HEAD_DIM:gw + (hd + 1) * HEAD_DIM]
        vn_ref[:, cols] = _layer_norm(_gelu_tanh(v), sgun_ref[hd:hd + 1, :]).astype(BF16)
    xm = z[:, 2 * gw:3 * gw]
    xm_ref[...] = xm
    xcr_ref[...] = _short_conv(xm, convw_ref, convb_ref)
    sg_ref[...] = jax.nn.sigmoid(z[:, 3 * gw:])


def _block_diag_dot(x, bd_ref):
    parts = [jnp.dot(x[:, i * MXU_WIDTH:(i + 1) * MXU_WIDTH], bd_ref[i], preferred_element_type=F32)
             for i in range(bd_ref.shape[0])]
    return jnp.concatenate(parts, axis=1)


def _mlstm_prepare(first, last, xm_ref, xprev_ref, xnext_ref, xcr_ref, convw_ref, convb_ref,
                   bdk_ref, bdqt_ref, bdvt_ref, wgx_ref, wgm_ref, bgate_ref,
                   xce_ref, qt_ref, k_ref, vt_ref, ga_ref, gr_ref):
    tile = xm_ref.shape[0]
    xm = xm_ref[...]
    prev = jnp.where(first, 0.0, xprev_ref[...])
    nxt = jnp.where(last, 0.0, xnext_ref[...])
    r = HALO_ROWS

    def edge_conv(above, rows, below):
        ext = jnp.concatenate([above, rows, below], axis=0)
        return _short_conv(ext, convw_ref, convb_ref)[r:2 * r]

    top = edge_conv(prev, xm[0:r], xm[r:2 * r])
    bottom = edge_conv(xm[tile - 2 * r:tile - r], xm[tile - r:], nxt)
    xce_ref[0:r, :] = top
    xce_ref[r:, :] = bottom
    xc = jnp.concatenate([top, xcr_ref[r:tile - r, :], bottom], axis=0)

    xc_b = xc.astype(BF16)
    xm_b = xm.astype(BF16)
    k_ref[...] = (_block_diag_dot(xc_b, bdk_ref) * (HEAD_DIM ** -0.5)).astype(BF16)
    gates = (jnp.dot(xc_b, wgx_ref[...], preferred_element_type=F32)
             + jnp.dot(xm_b, wgm_ref[...], preferred_element_type=F32)
             + bgate_ref[...])
    n_chunks = tile // CHUNK

    for i in range(bdvt_ref.shape[0]):
        blk = slice(i * MXU_WIDTH, (i + 1) * MXU_WIDTH)
        qt = lax.dot_general(bdqt_ref[i], xc_b[:, blk], _NT, preferred_element_type=F32).astype(BF16)
        vt = lax.dot_general(bdvt_ref[i], xm_b[:, blk], _NT, preferred_element_type=F32)
        for c in range(n_chunks):
            qt_ref[c, blk, :] = qt[:, c * CHUNK:(c + 1) * CHUNK]
            vt_ref[c, blk, :] = vt[:, c * CHUNK:(c + 1) * CHUNK]

    lane = lax.broadcasted_iota(jnp.int32, gates.shape, 1)
    is_forget = (((lane >= B_LANE[False]) & (lane < B_LANE[False] + HEADS))
                 | ((lane >= B_LANE[True]) & (lane < B_LANE[True] + HEADS)))
    mixed = jnp.where(is_forget, _log_sigmoid(gates), gates)
    gt = jnp.concatenate([mixed[c * CHUNK:(c + 1) * CHUNK].T[:GATE_ROWS] for c in range(n_chunks)], axis=0)
    n_rows = n_chunks * GATE_ROWS
    hi = gt.astype(BF16)
    rem = gt - hi.astype(F32)
    mid = rem.astype(BF16)
    pieces = jnp.concatenate([hi, mid, (rem - mid.astype(F32)).astype(BF16)], axis=0)
    src = lax.broadcasted_iota(jnp.int32, (CHUNK, CHUNK), 0)
    dst = lax.broadcasted_iota(jnp.int32, (CHUNK, CHUNK), 1)

    def prefix_sum(includes):
        parts = jnp.dot(pieces, includes.astype(BF16), preferred_element_type=F32)
        return parts[:n_rows] + parts[n_rows:2 * n_rows] + parts[2 * n_rows:]

    row = lax.broadcasted_iota(jnp.int32, gt.shape, 0) % GATE_ROWS
    gr = jnp.where((row >= B_LANE[False]) & (row < B_LANE[False] + HEADS), prefix_sum(src <= dst),
                   jnp.where((row >= B_LANE[True]) & (row < B_LANE[True] + HEADS), prefix_sum(src >= dst), gt))
    gr_ref[...] = gr
    a_rows = gr - pltpu.roll(gr, n_rows - HEADS, axis=0)
    pad = jnp.zeros((GATE_LANES - GATE_ROWS, CHUNK), F32)
    for c in range(n_chunks):
        a_c = jnp.concatenate([a_rows[c * GATE_ROWS:(c + 1) * GATE_ROWS], pad], axis=0)
        ga_ref[c * CHUNK:(c + 1) * CHUNK, :] = a_c.T


def _mlstm_direction(reverse, n_chunks, qt_ref, k_ref, vt_ref, ga_ref, gr_ref, state_s, m_s, out_ref):
    key = lax.broadcasted_iota(jnp.int32, (CHUNK, CHUNK), 0)
    qry = lax.broadcasted_iota(jnp.int32, (CHUNK, CHUNK), 1)
    visible = (key >= qry) if reverse else (key <= qry)
    first_row = lax.broadcasted_iota(jnp.int32, (STATE_ROWS - HEAD_DIM, CHUNK), 0) == 0
    edge = 0 if reverse else CHUNK - 1
    ig0, b0 = IG_LANE[reverse], B_LANE[reverse]

    order = [(n_chunks - 1 - i) if reverse else i for i in range(n_chunks)]
    head_cols = [slice(h * HEAD_DIM, (h + 1) * HEAD_DIM) for h in range(HEADS)]
    chunk_rows = [slice(c * CHUNK, (c + 1) * CHUNK) for c in range(n_chunks)]
    gr = [gr_ref[c * GATE_ROWS:(c + 1) * GATE_ROWS, :] for c in range(n_chunks)]
    b_rows = {(c, h): gr[c][b0 + h:b0 + h + 1, :] for c in order for h in range(HEADS)}

    incr = {}
    for c in order:
        for h in range(HEADS):
            b_row = b_rows[c, h]
            ig_row = gr[c][ig0 + h:ig0 + h + 1, :]
            b_edge = jnp.broadcast_to(b_row[:, edge:edge + 1], b_row.shape)
            dec = (b_edge - b_row) + ig_row
            dec_max = jnp.broadcast_to(jnp.max(dec, axis=1, keepdims=True), dec.shape)
            wk = jnp.exp(dec - dec_max)
            lhs = jnp.concatenate([vt_ref[c, head_cols[h], :] * wk, jnp.where(first_row, wk, 0.0)],
                                  axis=0).astype(BF16)
            update = jnp.dot(lhs, k_ref[chunk_rows[c], head_cols[h]], preferred_element_type=F32)
            incr[c, h] = (b_edge, dec_max, update)

    states = [state_s[h] for h in range(HEADS)]
    ms = [m_s[h:h + 1, :] for h in range(HEADS)]
    partial = {}
    for c in order:
        ga = ga_ref[chunk_rows[c], :]
        for h in range(HEADS):
            b_row = b_rows[c, h]
            a_col = ga[:, ig0 + h:ig0 + h + 1]
            state, m_prev = states[h], ms[h]
            d = jnp.where(visible, b_row + a_col, -jnp.inf)
            m_loc = jnp.max(d, axis=0, keepdims=True)
            lhs = jnp.concatenate([k_ref[chunk_rows[c], head_cols[h]], state.astype(BF16)], axis=0)
            both = jnp.dot(lhs, qt_ref[c, head_cols[h], :], preferred_element_type=F32)
            st = both[:CHUNK] * jnp.exp(d - m_loc)
            carried = both[CHUNK:]
            inter = b_row + m_prev
            m_j = jnp.maximum(inter, m_loc)
            w_inter = jnp.exp(inter - m_j)
            w_loc = jnp.exp(m_loc - m_j)
            nq = (w_inter * carried[HEAD_DIM:HEAD_DIM + 1]
                  + w_loc * jnp.sum(st, axis=0, keepdims=True))
            inv = 1.0 / jnp.maximum(jnp.abs(nq), jnp.exp(-m_j))
            partial[c, h] = (st.astype(BF16), w_inter * inv, w_loc * inv, carried[:HEAD_DIM])

            b_edge, dec_max, update = incr[c, h]
            m_new = jnp.maximum(b_edge + m_prev, dec_max)
            states[h] = (jnp.exp(b_edge + m_prev - m_new) * state
                         + jnp.exp(dec_max - m_new) * update)
            ms[h] = m_new
    for h in range(HEADS):
        state_s[h] = states[h]
        m_s[h:h + 1, :] = ms[h]

    for c in order:
        for h in range(HEADS):
            st_b, s_inter, s_loc, carried = partial[c, h]
            num_loc = jnp.dot(vt_ref[c, head_cols[h], :].astype(BF16), st_b, preferred_element_type=F32)
            out_ref[c, head_cols[h], :] = s_inter * carried + s_loc * num_loc


def _reset_state(state_s, m_s):
    state_s[...] = jnp.zeros_like(state_s)
    m_s[...] = jnp.zeros_like(m_s)


def _mix_bwd_kernel(n_tiles, xm_ref, xprev_ref, xnext_ref, xcr_ref, convw_ref, convb_ref, bdk_ref,
                    bdqt_ref, bdvt_ref, wgx_ref, wgm_ref, bgate_ref,
                    hbt_ref, xce_ref, qt_ref, k_ref, vt_ref, ga_ref, gr_ref, state_s, m_s):
    j = pl.program_id(1)
    seq_tile = n_tiles - 1 - j

    @pl.when(j == 0)
    def _():
        _reset_state(state_s, m_s)

    _mlstm_prepare(seq_tile == 0, seq_tile == n_tiles - 1, xm_ref, xprev_ref, xnext_ref, xcr_ref,
                   convw_ref, convb_ref, bdk_ref, bdqt_ref, bdvt_ref, wgx_ref, wgm_ref,
                   bgate_ref, xce_ref, qt_ref, k_ref, vt_ref, ga_ref, gr_ref)
    _mlstm_direction(True, xm_ref.shape[0] // CHUNK, qt_ref, k_ref, vt_ref, ga_ref, gr_ref,
                     state_s, m_s, hbt_ref)


def _mix_fwd_kernel(final_norm, ug_ref, vn_ref, sg_ref, xcr_ref, h_ref, hbt_ref, xce_ref, qt_ref, k_ref, vt_ref,
                    ga_ref, gr_ref, sguw_ref, sgub_ref, mhn_ref, skip_ref, wout_ref,
                    g2_ref, wg_ref, wu_ref, wd_ref, gf_ref, o_ref,
                    state_s, m_s, hft_s, y_s):
    tile = h_ref.shape[0]

    @pl.when(pl.program_id(1) == 0)
    def _():
        _reset_state(state_s, m_s)

    _mlstm_direction(False, tile // CHUNK, qt_ref, k_ref, vt_ref, ga_ref, gr_ref, state_s, m_s, hft_s)

    for h in range(HEADS):
        cols = slice(h * HEAD_DIM, (h + 1) * HEAD_DIM)
        for c in range(tile // CHUNK):
            rows = slice(c * CHUNK, (c + 1) * CHUNK)
            mixed = jnp.dot(sguw_ref[h], vn_ref[rows, cols], preferred_element_type=F32) + sgub_ref[h]
            y_s[rows, cols] = (ug_ref[rows, cols] * mixed).astype(BF16)

    n_chunks = tile // CHUNK
    for c in range(n_chunks):
        rows = slice(c * CHUNK, (c + 1) * CHUNK)
        for h in range(HEADS):
            cols = slice(h * HEAD_DIM, (h + 1) * HEAD_DIM)
            xc = xcr_ref[rows, cols]
            if c == 0:
                xc = jnp.concatenate([xce_ref[0:HALO_ROWS, cols], xc[HALO_ROWS:]], axis=0)
            if c == n_chunks - 1:
                xc = jnp.concatenate([xc[:CHUNK - HALO_ROWS], xce_ref[HALO_ROWS:, cols]], axis=0)
            ht = hft_s[c, cols, :] + hbt_ref[c, cols, :]
            mu = jnp.mean(ht, axis=0, keepdims=True)
            hc = ht - mu
            var = jnp.mean(hc * hc, axis=0, keepdims=True)
            hn = (hc * lax.rsqrt(var + EPS) * mhn_ref[h]).T
            ym = (hn + skip_ref[:, cols] * xc) * sg_ref[rows, cols]
            y_s[rows, GROUP_WIDTH + h * HEAD_DIM:GROUP_WIDTH + (h + 1) * HEAD_DIM] = ym.astype(BF16)

    h = h_ref[...] + jnp.dot(y_s[...], wout_ref[...], preferred_element_type=F32)

    hn = _rms_norm(h, g2_ref[...]).astype(BF16)
    x = h + 0.5 * _swiglu(hn, wg_ref, wu_ref, wd_ref)
    if final_norm:
        x = _rms_norm(x, gf_ref[...])
    o_ref[...] = x


def _compiler_params(grid_rank):
    return pltpu.CompilerParams(dimension_semantics=("arbitrary",) * grid_rank,
                                vmem_limit_bytes=VMEM_LIMIT_BYTES)


def _resident(shape, layer=None):
    nd = len(shape)
    if layer is None:
        return pl.BlockSpec(shape, lambda *_: (0,) * nd, pipeline_mode=pl.Buffered(1))
    return pl.BlockSpec((None,) + shape, lambda *_: (layer,) + (0,) * nd,
                        pipeline_mode=pl.Buffered(1))


def _ffn_in(x, l, g1, wg, wu, wd, g2, win, sgun, convw, convb):
    tokens, d = x.shape
    hidden = wg.shape[-1]
    zw = win.shape[-1]
    gw = GROUP_WIDTH
    assert zw == 4 * gw
    row = lambda i: (i, 0)
    group_spec = pl.BlockSpec((FFN_TILE, gw), row)
    group_f32 = jax.ShapeDtypeStruct((tokens, gw), F32)
    return pl.pallas_call(
        _ffn_in_kernel,
        grid=(tokens // FFN_TILE,),
        in_specs=[pl.BlockSpec((FFN_TILE, d), row),
                  _resident((1, d), l), _resident((d, hidden), l), _resident((d, hidden), l),
                  _resident((hidden, d), l), _resident((1, d), l), _resident((d, zw), l),
                  _resident((HEADS, HEAD_DIM), l), _resident((CONV_WIDTH, gw), l), _resident((1, gw), l)],
        out_specs=[pl.BlockSpec((FFN_TILE, d), row)] + [group_spec] * 5,
        out_shape=[jax.ShapeDtypeStruct((tokens, d), F32), group_f32,
                   jax.ShapeDtypeStruct((tokens, gw), BF16), group_f32, group_f32, group_f32],
        compiler_params=_compiler_params(1),
        name=f"ffn_in_{l}",
    )(x, g1, wg, wu, wd, g2, win, sgun, convw, convb)


def _state_scratch():
    return [pltpu.VMEM((HEADS, STATE_ROWS, HEAD_DIM), F32),
            pltpu.VMEM((8, 128), F32)]


def _tile_row(n_tiles, reverse):
    return lambda b, j: b * n_tiles + ((n_tiles - 1 - j) if reverse else j)


def _shared_specs(tile_row):
    gw = GROUP_WIDTH
    chunks = MIX_TILE // CHUNK
    rows2 = lambda b, j: (tile_row(b, j), 0)
    rows3 = lambda b, j: (tile_row(b, j), 0, 0)
    return [pl.BlockSpec((2 * HALO_ROWS, gw), rows2),
            pl.BlockSpec((chunks, gw, CHUNK), rows3),
            pl.BlockSpec((MIX_TILE, gw), rows2),
            pl.BlockSpec((chunks, gw, CHUNK), rows3),
            pl.BlockSpec((MIX_TILE, GATE_LANES), rows2),
            pl.BlockSpec((chunks * GATE_ROWS, CHUNK), rows2)]


def _mix_bwd(xm, xcr, l, batch, mw):
    tokens = xm.shape[0]
    n_tiles = tokens // batch // MIX_TILE
    tile_row = _tile_row(n_tiles, True)
    gw = GROUP_WIDTH
    halo_per_tile = MIX_TILE // HALO_ROWS
    last_halo = tokens // HALO_ROWS - 1
    n_diag = gw // MXU_WIDTH
    chunks = MIX_TILE // CHUNK
    shared_shapes = [jax.ShapeDtypeStruct((tokens // MIX_TILE * 2 * HALO_ROWS, gw), F32),
                     jax.ShapeDtypeStruct((tokens // CHUNK, gw, CHUNK), BF16),
                     jax.ShapeDtypeStruct((tokens, gw), BF16),
                     jax.ShapeDtypeStruct((tokens // CHUNK, gw, CHUNK), F32),
                     jax.ShapeDtypeStruct((tokens, GATE_LANES), F32),
                     jax.ShapeDtypeStruct((tokens // CHUNK * GATE_ROWS, CHUNK), F32)]
    return pl.pallas_call(
        functools.partial(_mix_bwd_kernel, n_tiles),
        grid=(batch, n_tiles),
        in_specs=[pl.BlockSpec((MIX_TILE, gw), lambda b, j: (tile_row(b, j), 0)),
                  pl.BlockSpec((HALO_ROWS, gw), lambda b, j: (
                      jnp.maximum(tile_row(b, j) * halo_per_tile - 1, 0), 0)),
                  pl.BlockSpec((HALO_ROWS, gw), lambda b, j: (
                      jnp.minimum((tile_row(b, j) + 1) * halo_per_tile, last_halo), 0)),
                  pl.BlockSpec((MIX_TILE, gw), lambda b, j: (tile_row(b, j), 0)),
                  _resident((CONV_WIDTH, gw), l), _resident((1, gw), l),
                  _resident((n_diag, MXU_WIDTH, MXU_WIDTH), l),
                  _resident((n_diag, MXU_WIDTH, MXU_WIDTH), l),
                  _resident((n_diag, MXU_WIDTH, MXU_WIDTH), l),
                  _resident((gw, GATE_LANES), l), _resident((gw, GATE_LANES), l),
                  _resident((1, GATE_LANES), l)],
        out_specs=[pl.BlockSpec((chunks, gw, CHUNK), lambda b, j: (tile_row(b, j), 0, 0))]
                  + _shared_specs(tile_row),
        out_shape=[jax.ShapeDtypeStruct((tokens // CHUNK, gw, CHUNK), F32)] + shared_shapes,
        scratch_shapes=_state_scratch(),
        compiler_params=_compiler_params(2),
        name=f"mix_bwd_{l}",
    )(xm, xm, xm, xcr, *mw)


def _mix_fwd(ug, vn, sg, xcr, h, hbt, shared, l, batch, sguw, sgub, mhn, skip, wout,
             g2, wg, wu, wd, gf, final_norm):
    tokens, d = h.shape
    hidden = wg.shape[-1]
    n_tiles = tokens // batch // MIX_TILE
    tile_row = _tile_row(n_tiles, False)
    gw = GROUP_WIDTH
    chunks = MIX_TILE // CHUNK
    rows2 = lambda b, j: (tile_row(b, j), 0)
    return pl.pallas_call(
        functools.partial(_mix_fwd_kernel, final_norm),
        grid=(batch, n_tiles),
        in_specs=[pl.BlockSpec((MIX_TILE, gw), rows2),
                  pl.BlockSpec((MIX_TILE, gw), rows2),
                  pl.BlockSpec((MIX_TILE, gw), rows2),
                  pl.BlockSpec((MIX_TILE, gw), rows2),
                  pl.BlockSpec((MIX_TILE, d), rows2),
                  pl.BlockSpec((chunks, gw, CHUNK), lambda b, j: (tile_row(b, j), 0, 0))]
                 + _shared_specs(tile_row)
                 + [_resident((HEADS, CHUNK, CHUNK), l),
                    _resident((HEADS, CHUNK, HEAD_DIM), l), _resident((HEADS, HEAD_DIM, CHUNK), l),
                    _resident((1, gw), l), _resident((2 * gw, d), l),
                    _resident((1, d), l), _resident((d, hidden), l), _resident((d, hidden), l),
                    _resident((hidden, d), l), _resident((1, d))],
        out_specs=pl.BlockSpec((MIX_TILE, d), rows2),
        out_shape=jax.ShapeDtypeStruct((tokens, d), F32),
        scratch_shapes=_state_scratch() + [pltpu.VMEM((chunks, gw, CHUNK), F32),
                                           pltpu.VMEM((MIX_TILE, 2 * gw), BF16)],
        compiler_params=_compiler_params(2),
        name=f"mix_fwd_{l}",
    )(ug, vn, sg, xcr, h, hbt, *shared, sguw, sgub, mhn, skip, wout, g2, wg, wu, wd, gf)


def _diag_blocks(w, transpose=False, dtype=BF16):
    depth, groups, bi, bo = w.shape
    rows = w.reshape(depth, groups * bi // MXU_WIDTH, MXU_WIDTH, bo)
    idx = jnp.arange(MXU_WIDTH)
    tiled = jnp.einsum('dnro,oc->dnrc', rows, (idx[None, :] % bo == jnp.arange(bo)[:, None]).astype(F32),
                       precision=lax.Precision.HIGHEST)
    dense = jnp.where(idx[:, None] // bi == idx[None, :] // bo, tiled, 0.0)
    if transpose:
        dense = jnp.swapaxes(dense, -1, -2)
    return dense.astype(dtype)


def _pack_gates(w_fwd, b_fwd, w_bwd, b_bwd):
    depth, rows, n = w_fwd.shape
    w = jnp.concatenate([w_fwd, w_bwd, jnp.zeros((depth, rows, GATE_LANES - 2 * n), F32)], axis=-1)
    b = jnp.concatenate([b_fwd, b_bwd, jnp.zeros((depth, GATE_LANES - 2 * n), F32)], axis=-1)
    return w, b.reshape(depth, 1, GATE_LANES)


def _fold_gate_weights(w_q, w_k, w_v, wgate):
    depth = wgate.shape[0]
    gw = GROUP_WIDTH

    def folded(w, rows):
        blocks = _diag_blocks(w, dtype=F32)
        wg = wgate[:, rows:rows + gw].reshape(depth, gw // MXU_WIDTH, MXU_WIDTH, GATE_LANES)
        return jnp.einsum('dnrc,dnck->dnrk', blocks, wg,
                          precision=lax.Precision.HIGHEST).reshape(depth, gw, GATE_LANES)

    wgx = folded(w_q, 0) + folded(w_k, gw) * (HEAD_DIM ** -0.5)
    return wgx.astype(BF16), folded(w_v, 2 * gw).astype(BF16)


def kernel(x, ffn1_norm, ffn1_w_gate, ffn1_w_up, ffn1_w_down, mix_norm, w_in, sgu_norm, sgu_w, sgu_b, conv_w, conv_b, w_q, w_k, w_v, gate_w_fwd, gate_b_fwd, gate_w_bwd, gate_b_bwd, mh_norm, mlstm_skip, w_out, ffn2_norm, ffn2_w_gate, ffn2_w_up, ffn2_w_down, final_norm):
    batch, seq, d = x.shape
    depth = w_in.shape[0]
    assert seq % MIX_TILE == 0 and (batch * seq) % FFN_TILE == 0
    assert sgu_w.shape[1:] == (HEADS, CHUNK, CHUNK) and mh_norm.shape[1:] == (HEADS, HEAD_DIM)
    assert gate_w_fwd.shape[1:] == (3 * GROUP_WIDTH, 2 * HEADS)

    bf = lambda w: w.astype(BF16)
    row3 = lambda g: g.reshape(depth, 1, -1)
    f1 = (row3(ffn1_norm), bf(ffn1_w_gate), bf(ffn1_w_up), bf(ffn1_w_down))
    f2 = (row3(ffn2_norm), bf(ffn2_w_gate), bf(ffn2_w_up), bf(ffn2_w_down))
    wgate, bgate = _pack_gates(gate_w_fwd, gate_b_fwd, gate_w_bwd, gate_b_bwd)
    convb = row3(conv_b)
    wgx, wgm = _fold_gate_weights(w_q, w_k, w_v, wgate)
    mw = (conv_w, convb, _diag_blocks(w_k), _diag_blocks(w_q, transpose=True),
          _diag_blocks(w_v, transpose=True), wgx, wgm, bgate)
    sgub = jnp.broadcast_to(sgu_b[..., None], sgu_b.shape + (HEAD_DIM,))
    mhn = jnp.broadcast_to(mh_norm[..., None], mh_norm.shape + (CHUNK,))
    gfin = final_norm.reshape(1, d)

    xt = x.reshape(batch * seq, d)
    for l in range(depth):
        h, ug, vn, xm, xcr, sg = _ffn_in(xt, l, *f1, row3(mix_norm), bf(w_in), sgu_norm, conv_w, convb)
        hbt, *shared = _mix_bwd(xm, xcr, l, batch, mw)
        xt = _mix_fwd(ug, vn, sg, xcr, h, hbt, shared, l, batch, bf(sgu_w), sgub, mhn,
                      row3(mlstm_skip), bf(w_out), *f2, gfin, final_norm=(l == depth - 1))
    return xt.reshape(batch, seq, d)
```

```python
import functools

import jax
import jax.numpy as jnp
from jax import lax
from jax.experimental import pallas as pl
from jax.experimental.pallas import tpu as pltpu

F32 = jnp.float32
BF16 = jnp.bfloat16

EPS = 1e-6
HEADS = 4
HEAD_DIM = 128
CHUNK = 128
GROUP_WIDTH = HEADS * HEAD_DIM
CONV_WIDTH = 5
QKV_BLOCK = 4
MXU_WIDTH = 256
GATE_LANES = 128
GATE_ROWS = 4 * HEADS
IG_LANE = {False: 0, True: 2 * HEADS}
B_LANE = {False: HEADS, True: 3 * HEADS}
STATE_ROWS = HEAD_DIM + 32
HALO_ROWS = 16

FFN_TILE = 512
MIX_TILE = 512
FFN_HIDDEN_CHUNK = 256
VMEM_LIMIT_BYTES = 56 * 1024 * 1024

_NT = (((1,), (1,)), ((), ()))


def _rms_norm(x, g):
    return x * lax.rsqrt(jnp.mean(x * x, axis=-1, keepdims=True) + EPS) * g


def _layer_norm(x, g):
    mu = jnp.mean(x, axis=-1, keepdims=True)
    xc = x - mu
    var = jnp.mean(xc * xc, axis=-1, keepdims=True)
    return xc * lax.rsqrt(var + EPS) * g


def _gelu_tanh(x):
    c2 = 2.0 * (2.0 / jnp.pi) ** 0.5 * 1.4426950408889634
    return x / (1.0 + jnp.exp2(x * (x * x * (-0.044715 * c2) - c2)))


def _log_sigmoid(x):
    return jnp.minimum(x, 0.0) - jnp.log1p(jnp.exp(-jnp.abs(x)))


def _swiglu(xn, wg_ref, wu_ref, wd_ref):
    hidden = wg_ref.shape[1]
    acc = None
    for lo in range(0, hidden, FFN_HIDDEN_CHUNK):
        hi = min(lo + FFN_HIDDEN_CHUNK, hidden)
        gate = jnp.dot(xn, wg_ref[:, lo:hi], preferred_element_type=F32)
        up = jnp.dot(xn, wu_ref[:, lo:hi], preferred_element_type=F32)
        act = (jax.nn.silu(gate) * up).astype(BF16)
        part = jnp.dot(act, wd_ref[lo:hi, :], preferred_element_type=F32)
        acc = part if acc is None else acc + part
    return acc


def _short_conv(rows, convw_ref, convb_ref):
    n = rows.shape[0]
    conv = None
    for tap in range(CONV_WIDTH):
        off = tap - CONV_WIDTH // 2
        shifted = rows if off == 0 else pltpu.roll(rows, (n - off) % n, axis=0)
        term = shifted * convw_ref[tap:tap + 1, :]
        conv = term if conv is None else conv + term
    return jax.nn.silu(conv + convb_ref[...])


def _ffn_in_kernel(x_ref, g1_ref, wg_ref, wu_ref, wd_ref, g2_ref, h_ref, hn_ref):
    x = x_ref[...]
    xn = _rms_norm(x, g1_ref[...]).astype(BF16)
    h = x + 0.5 * _swiglu(xn, wg_ref, wu_ref, wd_ref)
    h_ref[...] = h
    hn_ref[...] = _rms_norm(h, g2_ref[...]).astype(BF16)


def _block_diag_dot(x, bd_ref):
    parts = [jnp.dot(x[:, i * MXU_WIDTH:(i + 1) * MXU_WIDTH], bd_ref[i], preferred_element_type=F32)
             for i in range(bd_ref.shape[0])]
    return jnp.concatenate(parts, axis=1)


def _mlstm_prepare(first, last, hn_ref, hprev_ref, hnext_ref, win_ref, sgun_ref, convw_ref, convb_ref,
                   bdk_ref, bdqt_ref, bdvt_ref, wgx_ref, wgm_ref, bgate_ref,
                   ug_ref, vn_ref, sg_ref, xc_ref, qt_ref, k_ref, vt_ref, ga_ref, gr_ref):
    tile = hn_ref.shape[0]
    gw = GROUP_WIDTH
    r = HALO_ROWS
    hn = hn_ref[...]

    def z_group(rows, g):
        return jnp.dot(rows, win_ref[:, g * gw:(g + 1) * gw], preferred_element_type=F32)

    xm_ext = z_group(jnp.concatenate([hprev_ref[...], hn, hnext_ref[...]], axis=0), 2)
    xm = xm_ext[r:r + tile]
    ext = jnp.concatenate([jnp.where(first, 0.0, xm_ext[:r]), xm,
                           jnp.where(last, 0.0, xm_ext[r + tile:])], axis=0)
    xc = _short_conv(ext, convw_ref, convb_ref)[r:r + tile]
    xc_ref[...] = xc

    v = z_group(hn, 1)
    for hd in range(HEADS):
        cols = slice(hd * HEAD_DIM, (hd + 1) * HEAD_DIM)
        vn_ref[:, cols] = _layer_norm(_gelu_tanh(v[:, cols]), sgun_ref[hd:hd + 1, :]).astype(BF16)
    ug_ref[...] = _gelu_tanh(z_group(hn, 0))
    sg_ref[...] = jax.nn.sigmoid(z_group(hn, 3))

    xc_b = xc.astype(BF16)
    xm_b = xm.astype(BF16)
    k_ref[...] = (_block_diag_dot(xc_b, bdk_ref) * (HEAD_DIM ** -0.5)).astype(BF16)
    gates = (jnp.dot(xc_b, wgx_ref[...], preferred_element_type=F32)
             + jnp.dot(xm_b, wgm_ref[...], preferred_element_type=F32)
             + bgate_ref[...])
    n_chunks = tile // CHUNK

    for i in range(bdvt_ref.shape[0]):
        blk = slice(i * MXU_WIDTH, (i + 1) * MXU_WIDTH)
        qt = lax.dot_general(bdqt_ref[i], xc_b[:, blk], _NT, preferred_element_type=F32).astype(BF16)
        vt = lax.dot_general(bdvt_ref[i], xm_b[:, blk], _NT, preferred_element_type=F32)
        for c in range(n_chunks):
            qt_ref[c, blk, :] = qt[:, c * CHUNK:(c + 1) * CHUNK]
            vt_ref[c, blk, :] = vt[:, c * CHUNK:(c + 1) * CHUNK]

    lane = lax.broadcasted_iota(jnp.int32, gates.shape, 1)
    is_forget = (((lane >= B_LANE[False]) & (lane < B_LANE[False] + HEADS))
                 | ((lane >= B_LANE[True]) & (lane < B_LANE[True] + HEADS)))
    mixed = jnp.where(is_forget, _log_sigmoid(gates), gates)
    gt = jnp.concatenate([mixed[c * CHUNK:(c + 1) * CHUNK].T[:GATE_ROWS] for c in range(n_chunks)], axis=0)
    n_rows = n_chunks * GATE_ROWS
    hi = gt.astype(BF16)
    rem = gt - hi.astype(F32)
    mid = rem.astype(BF16)
    pieces = jnp.concatenate([hi, mid, (rem - mid.astype(F32)).astype(BF16)], axis=0)
    src = lax.broadcasted_iota(jnp.int32, (CHUNK, CHUNK), 0)
    dst = lax.broadcasted_iota(jnp.int32, (CHUNK, CHUNK), 1)

    def prefix_sum(includes):
        parts = jnp.dot(pieces, includes.astype(BF16), preferred_element_type=F32)
        return parts[:n_rows] + parts[n_rows:2 * n_rows] + parts[2 * n_rows:]

    row = lax.broadcasted_iota(jnp.int32, gt.shape, 0) % GATE_ROWS
    gr = jnp.where((row >= B_LANE[False]) & (row < B_LANE[False] + HEADS), prefix_sum(src <= dst),
                   jnp.where((row >= B_LANE[True]) & (row < B_LANE[True] + HEADS), prefix_sum(src >= dst), gt))
    gr_ref[...] = gr
    a_rows = gr - pltpu.roll(gr, n_rows - HEADS, axis=0)
    pad = jnp.zeros((GATE_LANES - GATE_ROWS, CHUNK), F32)
    for c in range(n_chunks):
        a_c = jnp.concatenate([a_rows[c * GATE_ROWS:(c + 1) * GATE_ROWS], pad], axis=0)
        ga_ref[c * CHUNK:(c + 1) * CHUNK, :] = a_c.T


def _mlstm_direction(reverse, n_chunks, qt_ref, k_ref, vt_ref, ga_ref, gr_ref, state_s, m_s, out_ref):
    key = lax.broadcasted_iota(jnp.int32, (CHUNK, CHUNK), 0)
    qry = lax.broadcasted_iota(jnp.int32, (CHUNK, CHUNK), 1)
    visible = (key >= qry) if reverse else (key <= qry)
    first_row = lax.broadcasted_iota(jnp.int32, (STATE_ROWS - HEAD_DIM, CHUNK), 0) == 0
    edge = 0 if reverse else CHUNK - 1
    ig0, b0 = IG_LANE[reverse], B_LANE[reverse]

    order = [(n_chunks - 1 - i) if reverse else i for i in range(n_chunks)]
    head_cols = [slice(h * HEAD_DIM, (h + 1) * HEAD_DIM) for h in range(HEADS)]
    chunk_rows = [slice(c * CHUNK, (c + 1) * CHUNK) for c in range(n_chunks)]
    gr = [gr_ref[c * GATE_ROWS:(c + 1) * GATE_ROWS, :] for c in range(n_chunks)]
    b_rows = {(c, h): gr[c][b0 + h:b0 + h + 1, :] for c in order for h in range(HEADS)}

    incr = {}
    for c in order:
        for h in range(HEADS):
            b_row = b_rows[c, h]
            ig_row = gr[c][ig0 + h:ig0 + h + 1, :]
            b_edge = jnp.broadcast_to(b_row[:, edge:edge + 1], b_row.shape)
            dec = (b_edge - b_row) + ig_row
            dec_max = jnp.broadcast_to(jnp.max(dec, axis=1, keepdims=True), dec.shape)
            wk = jnp.exp(dec - dec_max)
            lhs = jnp.concatenate([vt_ref[c, head_cols[h], :] * wk, jnp.where(first_row, wk, 0.0)],
                                  axis=0).astype(BF16)
            update = jnp.dot(lhs, k_ref[chunk_rows[c], head_cols[h]], preferred_element_type=F32)
            incr[c, h] = (b_edge, dec_max, update)

    states = [state_s[h] for h in range(HEADS)]
    ms = [m_s[h:h + 1, :] for h in range(HEADS)]
    partial = {}
    for c in order:
        ga = ga_ref[chunk_rows[c], :]
        for h in range(HEADS):
            b_row = b_rows[c, h]
            a_col = ga[:, ig0 + h:ig0 + h + 1]
            state, m_prev = states[h], ms[h]
            d = jnp.where(visible, b_row + a_col, -jnp.inf)
            m_loc = jnp.max(d, axis=0, keepdims=True)
            lhs = jnp.concatenate([k_ref[chunk_rows[c], head_cols[h]], state.astype(BF16)], axis=0)
            both = jnp.dot(lhs, qt_ref[c, head_cols[h], :], preferred_element_type=F32)
            st = both[:CHUNK] * jnp.exp(d - m_loc)
            carried = both[CHUNK:]
            inter = b_row + m_prev
            m_j = jnp.maximum(inter, m_loc)
            w_inter = jnp.exp(inter - m_j)
            w_loc = jnp.exp(m_loc - m_j)
            nq = (w_inter * carried[HEAD_DIM:HEAD_DIM + 1]
                  + w_loc * jnp.sum(st, axis=0, keepdims=True))
            inv = 1.0 / jnp.maximum(jnp.abs(nq), jnp.exp(-m_j))
            partial[c, h] = (st.astype(BF16), w_inter * inv, w_loc * inv, carried[:HEAD_DIM])

            b_edge, dec_max, update = incr[c, h]
            m_new = jnp.maximum(b_edge + m_prev, dec_max)
            states[h] = (jnp.exp(b_edge + m_prev - m_new) * state
                         + jnp.exp(dec_max - m_new) * update)
            ms[h] = m_new
    for h in range(HEADS):
        state_s[h] = states[h]
        m_s[h:h + 1, :] = ms[h]

    for c in order:
        for h in range(HEADS):
            st_b, s_inter, s_loc, carried = partial[c, h]
            num_loc = jnp.dot(vt_ref[c, head_cols[h], :].astype(BF16), st_b, preferred_element_type=F32)
            out_ref[c, head_cols[h], :] = s_inter * carried + s_loc * num_loc


def _reset_state(state_s, m_s):
    state_s[...] = jnp.zeros_like(state_s)
    m_s[...] = jnp.zeros_like(m_s)


def _mix_bwd_kernel(n_tiles, hn_ref, hprev_ref, hnext_ref, win_ref, sgun_ref, convw_ref, convb_ref, bdk_ref,
                    bdqt_ref, bdvt_ref, wgx_ref, wgm_ref, bgate_ref,
                    hbt_ref, ug_ref, vn_ref, sg_ref, xc_ref, qt_ref, k_ref, vt_ref, ga_ref, gr_ref,
                    state_s, m_s):
    j = pl.program_id(1)
    seq_tile = n_tiles - 1 - j

    @pl.when(j == 0)
    def _():
        _reset_state(state_s, m_s)

    _mlstm_prepare(seq_tile == 0, seq_tile == n_tiles - 1, hn_ref, hprev_ref, hnext_ref, win_ref, sgun_ref,
                   convw_ref, convb_ref, bdk_ref, bdqt_ref, bdvt_ref, wgx_ref, wgm_ref, bgate_ref,
                   ug_ref, vn_ref, sg_ref, xc_ref, qt_ref, k_ref, vt_ref, ga_ref, gr_ref)
    _mlstm_direction(True, hn_ref.shape[0] // CHUNK, qt_ref, k_ref, vt_ref, ga_ref, gr_ref,
                     state_s, m_s, hbt_ref)


def _mix_fwd_kernel(final_norm, h_ref, hbt_ref, ug_ref, vn_ref, sg_ref, xc_ref, qt_ref, k_ref, vt_ref,
                    ga_ref, gr_ref, sguw_ref, sgub_ref, mhn_ref, skip_ref, wout_ref,
                    g2_ref, wg_ref, wu_ref, wd_ref, gf_ref, o_ref,
                    state_s, m_s, hft_s, y_s):
    tile = h_ref.shape[0]

    @pl.when(pl.program_id(1) == 0)
    def _():
        _reset_state(state_s, m_s)

    _mlstm_direction(False, tile // CHUNK, qt_ref, k_ref, vt_ref, ga_ref, gr_ref, state_s, m_s, hft_s)

    for h in range(HEADS):
        cols = slice(h * HEAD_DIM, (h + 1) * HEAD_DIM)
        for c in range(tile // CHUNK):
            rows = slice(c * CHUNK, (c + 1) * CHUNK)
            mixed = jnp.dot(sguw_ref[h], vn_ref[rows, cols], preferred_element_type=F32) + sgub_ref[h]
            y_s[rows, cols] = (ug_ref[rows, cols] * mixed).astype(BF16)

    n_chunks = tile // CHUNK
    for c in range(n_chunks):
        rows = slice(c * CHUNK, (c + 1) * CHUNK)
        for h in range(HEADS):
            cols = slice(h * HEAD_DIM, (h + 1) * HEAD_DIM)
            ht = hft_s[c, cols, :] + hbt_ref[c, cols, :]
            mu = jnp.mean(ht, axis=0, keepdims=True)
            hc = ht - mu
            var = jnp.mean(hc * hc, axis=0, keepdims=True)
            hn = (hc * lax.rsqrt(var + EPS) * mhn_ref[h]).T
            ym = (hn + skip_ref[:, cols] * xc_ref[rows, cols]) * sg_ref[rows, cols]
            y_s[rows, GROUP_WIDTH + h * HEAD_DIM:GROUP_WIDTH + (h + 1) * HEAD_DIM] = ym.astype(BF16)

    h = h_ref[...] + jnp.dot(y_s[...], wout_ref[...], preferred_element_type=F32)

    hn = _rms_norm(h, g2_ref[...]).astype(BF16)
    x = h + 0.5 * _swiglu(hn, wg_ref, wu_ref, wd_ref)
    if final_norm:
        x = _rms_norm(x, gf_ref[...])
    o_ref[...] = x


def _compiler_params(grid_rank):
    return pltpu.CompilerParams(dimension_semantics=("arbitrary",) * grid_rank,
                                vmem_limit_bytes=VMEM_LIMIT_BYTES)


def _resident(shape, layer=None):
    nd = len(shape)
    if layer is None:
        return pl.BlockSpec(shape, lambda *_: (0,) * nd, pipeline_mode=pl.Buffered(1))
    return pl.BlockSpec((None,) + shape, lambda *_: (layer,) + (0,) * nd,
                        pipeline_mode=pl.Buffered(1))


def _ffn_in(x, l, g1, wg, wu, wd, g2):
    tokens, d = x.shape
    hidden = wg.shape[-1]
    row = lambda i: (i, 0)
    return pl.pallas_call(
        _ffn_in_kernel,
        grid=(tokens // FFN_TILE,),
        in_specs=[pl.BlockSpec((FFN_TILE, d), row),
                  _resident((1, d), l), _resident((d, hidden), l), _resident((d, hidden), l),
                  _resident((hidden, d), l), _resident((1, d), l)],
        out_specs=[pl.BlockSpec((FFN_TILE, d), row), pl.BlockSpec((FFN_TILE, d), row)],
        out_shape=[jax.ShapeDtypeStruct((tokens, d), F32), jax.ShapeDtypeStruct((tokens, d), BF16)],
        compiler_params=_compiler_params(1),
        name=f"ffn_in_{l}",
    )(x, g1, wg, wu, wd, g2)


def _state_scratch():
    return [pltpu.VMEM((HEADS, STATE_ROWS, HEAD_DIM), F32),
            pltpu.VMEM((8, 128), F32)]


def _tile_row(n_tiles, reverse):
    return lambda b, j: b * n_tiles + ((n_tiles - 1 - j) if reverse else j)


def _shared_specs(tile_row):
    gw = GROUP_WIDTH
    chunks = MIX_TILE // CHUNK
    rows2 = lambda b, j: (tile_row(b, j), 0)
    rows3 = lambda b, j: (tile_row(b, j), 0, 0)
    return [pl.BlockSpec((MIX_TILE, gw), rows2),
            pl.BlockSpec((MIX_TILE, gw), rows2),
            pl.BlockSpec((MIX_TILE, gw), rows2),
            pl.BlockSpec((MIX_TILE, gw), rows2),
            pl.BlockSpec((chunks, gw, CHUNK), rows3),
            pl.BlockSpec((MIX_TILE, gw), rows2),
            pl.BlockSpec((chunks, gw, CHUNK), rows3),
            pl.BlockSpec((MIX_TILE, GATE_LANES), rows2),
            pl.BlockSpec((chunks * GATE_ROWS, CHUNK), rows2)]


def _mix_bwd(hn, l, batch, win, sgun, mw):
    tokens, d = hn.shape
    n_tiles = tokens // batch // MIX_TILE
    tile_row = _tile_row(n_tiles, True)
    gw = GROUP_WIDTH
    halo_per_tile = MIX_TILE // HALO_ROWS
    last_halo = tokens // HALO_ROWS - 1
    n_diag = gw // MXU_WIDTH
    chunks = MIX_TILE // CHUNK
    token_f32 = jax.ShapeDtypeStruct((tokens, gw), F32)
    shared_shapes = [token_f32, jax.ShapeDtypeStruct((tokens, gw), BF16), token_f32, token_f32,
                     jax.ShapeDtypeStruct((tokens // CHUNK, gw, CHUNK), BF16),
                     jax.ShapeDtypeStruct((tokens, gw), BF16),
                     jax.ShapeDtypeStruct((tokens // CHUNK, gw, CHUNK), F32),
                     jax.ShapeDtypeStruct((tokens, GATE_LANES), F32),
                     jax.ShapeDtypeStruct((tokens // CHUNK * GATE_ROWS, CHUNK), F32)]
    return pl.pallas_call(
        functools.partial(_mix_bwd_kernel, n_tiles),
        grid=(batch, n_tiles),
        in_specs=[pl.BlockSpec((MIX_TILE, d), lambda b, j: (tile_row(b, j), 0)),
                  pl.BlockSpec((HALO_ROWS, d), lambda b, j: (
                      jnp.maximum(tile_row(b, j) * halo_per_tile - 1, 0), 0)),
                  pl.BlockSpec((HALO_ROWS, d), lambda b, j: (
                      jnp.minimum((tile_row(b, j) + 1) * halo_per_tile, last_halo), 0)),
                  _resident((d, 4 * gw), l), _resident((HEADS, HEAD_DIM), l),
                  _resident((CONV_WIDTH, gw), l), _resident((1, gw), l),
                  _resident((n_diag, MXU_WIDTH, MXU_WIDTH), l),
                  _resident((n_diag, MXU_WIDTH, MXU_WIDTH), l),
                  _resident((n_diag, MXU_WIDTH, MXU_WIDTH), l),
                  _resident((gw, GATE_LANES), l), _resident((gw, GATE_LANES), l),
                  _resident((1, GATE_LANES), l)],
        out_specs=[pl.BlockSpec((chunks, gw, CHUNK), lambda b, j: (tile_row(b, j), 0, 0))]
                  + _shared_specs(tile_row),
        out_shape=[jax.ShapeDtypeStruct((tokens // CHUNK, gw, CHUNK), F32)] + shared_shapes,
        scratch_shapes=_state_scratch(),
        compiler_params=_compiler_params(2),
        name=f"mix_bwd_{l}",
    )(hn, hn, hn, win, sgun, *mw)


def _mix_fwd(h, hbt, shared, l, batch, sguw, sgub, mhn, skip, wout,
             g2, wg, wu, wd, gf, final_norm):
    tokens, d = h.shape
    hidden = wg.shape[-1]
    n_tiles = tokens // batch // MIX_TILE
    tile_row = _tile_row(n_tiles, False)
    gw = GROUP_WIDTH
    chunks = MIX_TILE // CHUNK
    rows2 = lambda b, j: (tile_row(b, j), 0)
    return pl.pallas_call(
        functools.partial(_mix_fwd_kernel, final_norm),
        grid=(batch, n_tiles),
        in_specs=[pl.BlockSpec((MIX_TILE, d), rows2),
                  pl.BlockSpec((chunks, gw, CHUNK), lambda b, j: (tile_row(b, j), 0, 0))]
                 + _shared_specs(tile_row)
                 + [_resident((HEADS, CHUNK, CHUNK), l),
                    _resident((HEADS, CHUNK, HEAD_DIM), l), _resident((HEADS, HEAD_DIM, CHUNK), l),
                    _resident((1, gw), l), _resident((2 * gw, d), l),
                    _resident((1, d), l), _resident((d, hidden), l), _resident((d, hidden), l),
                    _resident((hidden, d), l), _resident((1, d))],
        out_specs=pl.BlockSpec((MIX_TILE, d), rows2),
        out_shape=jax.ShapeDtypeStruct((tokens, d), F32),
        scratch_shapes=_state_scratch() + [pltpu.VMEM((chunks, gw, CHUNK), F32),
                                           pltpu.VMEM((MIX_TILE, 2 * gw), BF16)],
        compiler_params=_compiler_params(2),
        name=f"mix_fwd_{l}",
    )(h, hbt, *shared, sguw, sgub, mhn, skip, wout, g2, wg, wu, wd, gf)


def _diag_blocks(w, transpose=False, dtype=BF16):
    depth, groups, bi, bo = w.shape
    rows = w.reshape(depth, groups * bi // MXU_WIDTH, MXU_WIDTH, bo)
    idx = jnp.arange(MXU_WIDTH)
    tiled = jnp.einsum('dnro,oc->dnrc', rows, (idx[None, :] % bo == jnp.arange(bo)[:, None]).astype(F32),
                       precision=lax.Precision.HIGHEST)
    dense = jnp.where(idx[:, None] // bi == idx[None, :] // bo, tiled, 0.0)
    if transpose:
        dense = jnp.swapaxes(dense, -1, -2)
    return dense.astype(dtype)


def _pack_gates(w_fwd, b_fwd, w_bwd, b_bwd):
    depth, rows, n = w_fwd.shape
    w = jnp.concatenate([w_fwd, w_bwd, jnp.zeros((depth, rows, GATE_LANES - 2 * n), F32)], axis=-1)
    b = jnp.concatenate([b_fwd, b_bwd, jnp.zeros((depth, GATE_LANES - 2 * n), F32)], axis=-1)
    return w, b.reshape(depth, 1, GATE_LANES)


def _fold_gate_weights(w_q, w_k, w_v, wgate):
    depth = wgate.shape[0]
    gw = GROUP_WIDTH

    def folded(w, rows):
        blocks = _diag_blocks(w, dtype=F32)
        wg = wgate[:, rows:rows + gw].reshape(depth, gw // MXU_WIDTH, MXU_WIDTH, GATE_LANES)
        return jnp.einsum('dnrc,dnck->dnrk', blocks, wg,
                          precision=lax.Precision.HIGHEST).reshape(depth, gw, GATE_LANES)

    wgx = folded(w_q, 0) + folded(w_k, gw) * (HEAD_DIM ** -0.5)
    return wgx.astype(BF16), folded(w_v, 2 * gw).astype(BF16)


def kernel(x, ffn1_norm, ffn1_w_gate, ffn1_w_up, ffn1_w_down, mix_norm, w_in, sgu_norm, sgu_w, sgu_b, conv_w, conv_b, w_q, w_k, w_v, gate_w_fwd, gate_b_fwd, gate_w_bwd, gate_b_bwd, mh_norm, mlstm_skip, w_out, ffn2_norm, ffn2_w_gate, ffn2_w_up, ffn2_w_down, final_norm):
    batch, seq, d = x.shape
    depth = w_in.shape[0]
    assert seq % MIX_TILE == 0 and (batch * seq) % FFN_TILE == 0
    assert sgu_w.shape[1:] == (HEADS, CHUNK, CHUNK) and mh_norm.shape[1:] == (HEADS, HEAD_DIM)
    assert gate_w_fwd.shape[1:] == (3 * GROUP_WIDTH, 2 * HEADS)

    bf = lambda w: w.astype(BF16)
    row3 = lambda g: g.reshape(depth, 1, -1)
    f1 = (row3(ffn1_norm), bf(ffn1_w_gate), bf(ffn1_w_up), bf(ffn1_w_down))
    f2 = (row3(ffn2_norm), bf(ffn2_w_gate), bf(ffn2_w_up), bf(ffn2_w_down))
    wgate, bgate = _pack_gates(gate_w_fwd, gate_b_fwd, gate_w_bwd, gate_b_bwd)
    convb = row3(conv_b)
    wgx, wgm = _fold_gate_weights(w_q, w_k, w_v, wgate)
    mw = (conv_w, convb, _diag_blocks(w_k), _diag_blocks(w_q, transpose=True),
          _diag_blocks(w_v, transpose=True), wgx, wgm, bgate)
    sgub = jnp.broadcast_to(sgu_b[..., None], sgu_b.shape + (HEAD_DIM,))
    mhn = jnp.broadcast_to(mh_norm[..., None], mh_norm.shape + (CHUNK,))
    gfin = final_norm.reshape(1, d)

    xt = x.reshape(batch * seq, d)
    for l in range(depth):
        h, hn = _ffn_in(xt, l, *f1, row3(mix_norm))
        hbt, *shared = _mix_bwd(hn, l, batch, bf(w_in), sgu_norm, mw)
        xt = _mix_fwd(h, hbt, shared, l, batch, bf(sgu_w), sgub, mhn,
                      row3(mlstm_skip), bf(w_out), *f2, gfin, final_norm=(l == depth - 1))
    return xt.reshape(batch, seq, d)
```

```python
import functools

import jax
import jax.numpy as jnp
from jax import lax
from jax.experimental import pallas as pl
from jax.experimental.pallas import tpu as pltpu

F32 = jnp.float32
BF16 = jnp.bfloat16

EPS = 1e-6
HEADS = 4
HEAD_DIM = 128
CHUNK = 128
GROUP_WIDTH = HEADS * HEAD_DIM
CONV_WIDTH = 5
QKV_BLOCK = 4
MXU_WIDTH = 256
GATE_LANES = 128
GATE_ROWS = 4 * HEADS
IG_LANE = {False: 0, True: 2 * HEADS}
B_LANE = {False: HEADS, True: 3 * HEADS}
STATE_ROWS = HEAD_DIM + 32
HALO_ROWS = 16

FFN_TILE = 1024
MIX_TILE = 512
MIX_BWD_TILE = 1024
FFN_HIDDEN_CHUNK = 256
VMEM_LIMIT_BYTES = 56 * 1024 * 1024

_NT = (((1,), (1,)), ((), ()))


def _rms_norm(x, g):
    return x * lax.rsqrt(jnp.mean(x * x, axis=-1, keepdims=True) + EPS) * g


def _layer_norm(x, g):
    mu = jnp.mean(x, axis=-1, keepdims=True)
    xc = x - mu
    var = jnp.mean(xc * xc, axis=-1, keepdims=True)
    return xc * lax.rsqrt(var + EPS) * g


def _gelu_tanh(x):
    c2 = 2.0 * (2.0 / jnp.pi) ** 0.5 * 1.4426950408889634
    return x / (1.0 + jnp.exp2(x * (x * x * (-0.044715 * c2) - c2)))


def _log_sigmoid(x):
    return jnp.minimum(x, 0.0) - jnp.log1p(jnp.exp(-jnp.abs(x)))


def _swiglu(xn, wg_ref, wu_ref, wd_ref):
    hidden = wg_ref.shape[1]
    acc = None
    for lo in range(0, hidden, FFN_HIDDEN_CHUNK):
        hi = min(lo + FFN_HIDDEN_CHUNK, hidden)
        gate = jnp.dot(xn, wg_ref[:, lo:hi], preferred_element_type=F32)
        up = jnp.dot(xn, wu_ref[:, lo:hi], preferred_element_type=F32)
        act = (jax.nn.silu(gate) * up).astype(BF16)
        part = jnp.dot(act, wd_ref[lo:hi, :], preferred_element_type=F32)
        acc = part if acc is None else acc + part
    return acc


def _short_conv(rows, convw_ref, convb_ref):
    n = rows.shape[0]
    conv = None
    for tap in range(CONV_WIDTH):
        off = tap - CONV_WIDTH // 2
        shifted = rows if off == 0 else pltpu.roll(rows, (n - off) % n, axis=0)
        term = shifted * convw_ref[tap:tap + 1, :]
        conv = term if conv is None else conv + term
    return jax.nn.silu(conv + convb_ref[...])


def _ffn_in_kernel(x_ref, g1_ref, wg_ref, wu_ref, wd_ref, g2_ref, h_ref, hn_ref):
    x = x_ref[...]
    xn = _rms_norm(x, g1_ref[...]).astype(BF16)
    h = x + 0.5 * _swiglu(xn, wg_ref, wu_ref, wd_ref)
    h_ref[...] = h
    hn_ref[...] = _rms_norm(h, g2_ref[...]).astype(BF16)


def _block_diag_dot(x, bd_ref):
    parts = [jnp.dot(x[:, i * MXU_WIDTH:(i + 1) * MXU_WIDTH], bd_ref[i], preferred_element_type=F32)
             for i in range(bd_ref.shape[0])]
    return jnp.concatenate(parts, axis=1)


def _mlstm_prepare(first, last, hn_ref, hprev_ref, hnext_ref, win_ref, sgun_ref, convw_ref, convb_ref,
                   bdk_ref, bdqt_ref, bdvt_ref, wgx_ref, wgm_ref, bgate_ref,
                   ug_ref, vn_ref, sg_ref, xc_ref, qt_ref, k_ref, vt_ref, ga_ref, gr_ref):
    tile = hn_ref.shape[0]
    gw = GROUP_WIDTH
    r = HALO_ROWS
    hn = hn_ref[...]

    def z_group(rows, g):
        return jnp.dot(rows, win_ref[:, g * gw:(g + 1) * gw], preferred_element_type=F32)

    xm_ext = z_group(jnp.concatenate([hprev_ref[...], hn, hnext_ref[...]], axis=0), 2)
    xm = xm_ext[r:r + tile]
    ext = jnp.concatenate([jnp.where(first, 0.0, xm_ext[:r]), xm,
                           jnp.where(last, 0.0, xm_ext[r + tile:])], axis=0)
    xc = _short_conv(ext, convw_ref, convb_ref)[r:r + tile]
    xc_ref[...] = xc

    v = z_group(hn, 1)
    for hd in range(HEADS):
        cols = slice(hd * HEAD_DIM, (hd + 1) * HEAD_DIM)
        vn_ref[:, cols] = _layer_norm(_gelu_tanh(v[:, cols]), sgun_ref[hd:hd + 1, :]).astype(BF16)
    ug_ref[...] = _gelu_tanh(z_group(hn, 0))
    sg_ref[...] = jax.nn.sigmoid(z_group(hn, 3))

    xc_b = xc.astype(BF16)
    xm_b = xm.astype(BF16)
    k_ref[...] = (_block_diag_dot(xc_b, bdk_ref) * (HEAD_DIM ** -0.5)).astype(BF16)
    gates = (jnp.dot(xc_b, wgx_ref[...], preferred_element_type=F32)
             + jnp.dot(xm_b, wgm_ref[...], preferred_element_type=F32)
             + bgate_ref[...])
    n_chunks = tile // CHUNK

    for i in range(bdvt_ref.shape[0]):
        blk = slice(i * MXU_WIDTH, (i + 1) * MXU_WIDTH)
        qt = lax.dot_general(bdqt_ref[i], xc_b[:, blk], _NT, preferred_element_type=F32).astype(BF16)
        vt = lax.dot_general(bdvt_ref[i], xm_b[:, blk], _NT, preferred_element_type=F32)
        for c in range(n_chunks):
            qt_ref[c, blk, :] = qt[:, c * CHUNK:(c + 1) * CHUNK]
            vt_ref[c, blk, :] = vt[:, c * CHUNK:(c + 1) * CHUNK]

    lane = lax.broadcasted_iota(jnp.int32, gates.shape, 1)
    is_forget = (((lane >= B_LANE[False]) & (lane < B_LANE[False] + HEADS))
                 | ((lane >= B_LANE[True]) & (lane < B_LANE[True] + HEADS)))
    mixed = jnp.where(is_forget, _log_sigmoid(gates), gates)
    gt = jnp.concatenate([mixed[c * CHUNK:(c + 1) * CHUNK].T[:GATE_ROWS] for c in range(n_chunks)], axis=0)
    n_rows = n_chunks * GATE_ROWS
    hi = gt.astype(BF16)
    rem = gt - hi.astype(F32)
    mid = rem.astype(BF16)
    pieces = jnp.concatenate([hi, mid, (rem - mid.astype(F32)).astype(BF16)], axis=0)
    src = lax.broadcasted_iota(jnp.int32, (CHUNK, CHUNK), 0)
    dst = lax.broadcasted_iota(jnp.int32, (CHUNK, CHUNK), 1)

    def prefix_sum(includes):
        parts = jnp.dot(pieces, includes.astype(BF16), preferred_element_type=F32)
        return parts[:n_rows] + parts[n_rows:2 * n_rows] + parts[2 * n_rows:]

    row = lax.broadcasted_iota(jnp.int32, gt.shape, 0) % GATE_ROWS
    gr = jnp.where((row >= B_LANE[False]) & (row < B_LANE[False] + HEADS), prefix_sum(src <= dst),
                   jnp.where((row >= B_LANE[True]) & (row < B_LANE[True] + HEADS), prefix_sum(src >= dst), gt))
    gr_ref[...] = gr
    a_rows = gr - pltpu.roll(gr, n_rows - HEADS, axis=0)
    pad = jnp.zeros((GATE_LANES - GATE_ROWS, CHUNK), F32)
    for c in range(n_chunks):
        a_c = jnp.concatenate([a_rows[c * GATE_ROWS:(c + 1) * GATE_ROWS], pad], axis=0)
        ga_ref[c * CHUNK:(c + 1) * CHUNK, :] = a_c.T


def _mlstm_direction(reverse, n_chunks, qt_ref, k_ref, vt_ref, ga_ref, gr_ref, state_s, m_s, out_ref):
    key = lax.broadcasted_iota(jnp.int32, (CHUNK, CHUNK), 0)
    qry = lax.broadcasted_iota(jnp.int32, (CHUNK, CHUNK), 1)
    visible = (key >= qry) if reverse else (key <= qry)
    first_row = lax.broadcasted_iota(jnp.int32, (STATE_ROWS - HEAD_DIM, CHUNK), 0) == 0
    edge = 0 if reverse else CHUNK - 1
    ig0, b0 = IG_LANE[reverse], B_LANE[reverse]

    order = [(n_chunks - 1 - i) if reverse else i for i in range(n_chunks)]
    head_cols = [slice(h * HEAD_DIM, (h + 1) * HEAD_DIM) for h in range(HEADS)]
    chunk_rows = [slice(c * CHUNK, (c + 1) * CHUNK) for c in range(n_chunks)]
    gr = [gr_ref[c * GATE_ROWS:(c + 1) * GATE_ROWS, :] for c in range(n_chunks)]
    b_rows = {(c, h): gr[c][b0 + h:b0 + h + 1, :] for c in order for h in range(HEADS)}

    incr = {}
    for c in order:
        for h in range(HEADS):
            b_row = b_rows[c, h]
            ig_row = gr[c][ig0 + h:ig0 + h + 1, :]
            b_edge = jnp.broadcast_to(b_row[:, edge:edge + 1], b_row.shape)
            dec = (b_edge - b_row) + ig_row
            dec_max = jnp.broadcast_to(jnp.max(dec, axis=1, keepdims=True), dec.shape)
            wk = jnp.exp(dec - dec_max)
            lhs = jnp.concatenate([vt_ref[c, head_cols[h], :] * wk, jnp.where(first_row, wk, 0.0)],
                                  axis=0).astype(BF16)
            update = jnp.dot(lhs, k_ref[chunk_rows[c], head_cols[h]], preferred_element_type=F32)
            incr[c, h] = (b_edge, dec_max, update)

    states = [state_s[h] for h in range(HEADS)]
    ms = [m_s[h:h + 1, :] for h in range(HEADS)]
    partial = {}
    for c in order:
        ga = ga_ref[chunk_rows[c], :]
        for h in range(HEADS):
            b_row = b_rows[c, h]
            a_col = ga[:, ig0 + h:ig0 + h + 1]
            state, m_prev = states[h], ms[h]
            d = jnp.where(visible, b_row + a_col, -jnp.inf)
            m_loc = jnp.max(d, axis=0, keepdims=True)
            lhs = jnp.concatenate([k_ref[chunk_rows[c], head_cols[h]], state.astype(BF16)], axis=0)
            both = jnp.dot(lhs, qt_ref[c, head_cols[h], :], preferred_element_type=F32)
            st = both[:CHUNK] * jnp.exp(d - m_loc)
            carried = both[CHUNK:]
            inter = b_row + m_prev
            m_j = jnp.maximum(inter, m_loc)
            w_inter = jnp.exp(inter - m_j)
            w_loc = jnp.exp(m_loc - m_j)
            nq = (w_inter * carried[HEAD_DIM:HEAD_DIM + 1]
                  + w_loc * jnp.sum(st, axis=0, keepdims=True))
            inv = 1.0 / jnp.maximum(jnp.abs(nq), jnp.exp(-m_j))
            partial[c, h] = (st.astype(BF16), w_inter * inv, w_loc * inv, carried[:HEAD_DIM])

            b_edge, dec_max, update = incr[c, h]
            m_new = jnp.maximum(b_edge + m_prev, dec_max)
            states[h] = (jnp.exp(b_edge + m_prev - m_new) * state
                         + jnp.exp(dec_max - m_new) * update)
            ms[h] = m_new
    for h in range(HEADS):
        state_s[h] = states[h]
        m_s[h:h + 1, :] = ms[h]

    for c in order:
        for h in range(HEADS):
            st_b, s_inter, s_loc, carried = partial[c, h]
            num_loc = jnp.dot(vt_ref[c, head_cols[h], :].astype(BF16), st_b, preferred_element_type=F32)
            out_ref[c, head_cols[h], :] = s_inter * carried + s_loc * num_loc


def _reset_state(state_s, m_s):
    state_s[...] = jnp.zeros_like(state_s)
    m_s[...] = jnp.zeros_like(m_s)


def _mix_bwd_kernel(n_tiles, hn_ref, hprev_ref, hnext_ref, win_ref, sgun_ref, convw_ref, convb_ref, bdk_ref,
                    bdqt_ref, bdvt_ref, wgx_ref, wgm_ref, bgate_ref,
                    hbt_ref, ug_ref, vn_ref, sg_ref, xc_ref, qt_ref, k_ref, vt_ref, ga_ref, gr_ref,
                    state_s, m_s):
    j = pl.program_id(1)
    seq_tile = n_tiles - 1 - j

    @pl.when(j == 0)
    def _():
        _reset_state(state_s, m_s)

    _mlstm_prepare(seq_tile == 0, seq_tile == n_tiles - 1, hn_ref, hprev_ref, hnext_ref, win_ref, sgun_ref,
                   convw_ref, convb_ref, bdk_ref, bdqt_ref, bdvt_ref, wgx_ref, wgm_ref, bgate_ref,
                   ug_ref, vn_ref, sg_ref, xc_ref, qt_ref, k_ref, vt_ref, ga_ref, gr_ref)
    _mlstm_direction(True, hn_ref.shape[0] // CHUNK, qt_ref, k_ref, vt_ref, ga_ref, gr_ref,
                     state_s, m_s, hbt_ref)


def _mix_fwd_kernel(final_norm, h_ref, hbt_ref, ug_ref, vn_ref, sg_ref, xc_ref, qt_ref, k_ref, vt_ref,
                    ga_ref, gr_ref, sguw_ref, sgub_ref, mhn_ref, skip_ref, wout_ref,
                    g2_ref, wg_ref, wu_ref, wd_ref, gf_ref, o_ref,
                    state_s, m_s, hft_s, y_s):
    tile = h_ref.shape[0]

    @pl.when(pl.program_id(1) == 0)
    def _():
        _reset_state(state_s, m_s)

    _mlstm_direction(False, tile // CHUNK, qt_ref, k_ref, vt_ref, ga_ref, gr_ref, state_s, m_s, hft_s)

    for h in range(HEADS):
        cols = slice(h * HEAD_DIM, (h + 1) * HEAD_DIM)
        for c in range(tile // CHUNK):
            rows = slice(c * CHUNK, (c + 1) * CHUNK)
            mixed = jnp.dot(sguw_ref[h], vn_ref[rows, cols], preferred_element_type=F32) + sgub_ref[h]
            y_s[rows, cols] = (ug_ref[rows, cols] * mixed).astype(BF16)

    n_chunks = tile // CHUNK
    for c in range(n_chunks):
        rows = slice(c * CHUNK, (c + 1) * CHUNK)
        for h in range(HEADS):
            cols = slice(h * HEAD_DIM, (h + 1) * HEAD_DIM)
            ht = hft_s[c, cols, :] + hbt_ref[c, cols, :]
            mu = jnp.mean(ht, axis=0, keepdims=True)
            hc = ht - mu
            var = jnp.mean(hc * hc, axis=0, keepdims=True)
            hn = (hc * lax.rsqrt(var + EPS) * mhn_ref[h]).T
            ym = (hn + skip_ref[:, cols] * xc_ref[rows, cols]) * sg_ref[rows, cols]
            y_s[rows, GROUP_WIDTH + h * HEAD_DIM:GROUP_WIDTH + (h + 1) * HEAD_DIM] = ym.astype(BF16)

    h = h_ref[...] + jnp.dot(y_s[...], wout_ref[...], preferred_element_type=F32)

    hn = _rms_norm(h, g2_ref[...]).astype(BF16)
    x = h + 0.5 * _swiglu(hn, wg_ref, wu_ref, wd_ref)
    if final_norm:
        x = _rms_norm(x, gf_ref[...])
    o_ref[...] = x


def _compiler_params(grid_rank):
    return pltpu.CompilerParams(dimension_semantics=("arbitrary",) * grid_rank,
                                vmem_limit_bytes=VMEM_LIMIT_BYTES)


def _resident(shape, layer=None):
    nd = len(shape)
    if layer is None:
        return pl.BlockSpec(shape, lambda *_: (0,) * nd, pipeline_mode=pl.Buffered(1))
    return pl.BlockSpec((None,) + shape, lambda *_: (layer,) + (0,) * nd,
                        pipeline_mode=pl.Buffered(1))


def _ffn_in(x, l, g1, wg, wu, wd, g2):
    tokens, d = x.shape
    hidden = wg.shape[-1]
    row = lambda i: (i, 0)
    return pl.pallas_call(
        _ffn_in_kernel,
        grid=(tokens // FFN_TILE,),
        in_specs=[pl.BlockSpec((FFN_TILE, d), row),
                  _resident((1, d), l), _resident((d, hidden), l), _resident((d, hidden), l),
                  _resident((hidden, d), l), _resident((1, d), l)],
        out_specs=[pl.BlockSpec((FFN_TILE, d), row), pl.BlockSpec((FFN_TILE, d), row)],
        out_shape=[jax.ShapeDtypeStruct((tokens, d), F32), jax.ShapeDtypeStruct((tokens, d), BF16)],
        compiler_params=_compiler_params(1),
        name=f"ffn_in_{l}",
    )(x, g1, wg, wu, wd, g2)


def _state_scratch():
    return [pltpu.VMEM((HEADS, STATE_ROWS, HEAD_DIM), F32),
            pltpu.VMEM((8, 128), F32)]


def _tile_row(n_tiles, reverse):
    return lambda b, j: b * n_tiles + ((n_tiles - 1 - j) if reverse else j)


def _shared_specs(tile_row, tile):
    gw = GROUP_WIDTH
    chunks = tile // CHUNK
    rows2 = lambda b, j: (tile_row(b, j), 0)
    rows3 = lambda b, j: (tile_row(b, j), 0, 0)
    return [pl.BlockSpec((tile, gw), rows2),
            pl.BlockSpec((tile, gw), rows2),
            pl.BlockSpec((tile, gw), rows2),
            pl.BlockSpec((tile, gw), rows2),
            pl.BlockSpec((chunks, gw, CHUNK), rows3),
            pl.BlockSpec((tile, gw), rows2),
            pl.BlockSpec((chunks, gw, CHUNK), rows3),
            pl.BlockSpec((tile, GATE_LANES), rows2),
            pl.BlockSpec((chunks * GATE_ROWS, CHUNK), rows2)]


def _mix_bwd(hn, l, batch, win, sgun, mw):
    tokens, d = hn.shape
    tile = MIX_BWD_TILE
    n_tiles = tokens // batch // tile
    tile_row = _tile_row(n_tiles, True)
    gw = GROUP_WIDTH
    halo_per_tile = tile // HALO_ROWS
    last_halo = tokens // HALO_ROWS - 1
    n_diag = gw // MXU_WIDTH
    chunks = tile // CHUNK
    token_f32 = jax.ShapeDtypeStruct((tokens, gw), F32)
    shared_shapes = [token_f32, jax.ShapeDtypeStruct((tokens, gw), BF16), token_f32, token_f32,
                     jax.ShapeDtypeStruct((tokens // CHUNK, gw, CHUNK), BF16),
                     jax.ShapeDtypeStruct((tokens, gw), BF16),
                     jax.ShapeDtypeStruct((tokens // CHUNK, gw, CHUNK), F32),
                     jax.ShapeDtypeStruct((tokens, GATE_LANES), F32),
                     jax.ShapeDtypeStruct((tokens // CHUNK * GATE_ROWS, CHUNK), F32)]
    return pl.pallas_call(
        functools.partial(_mix_bwd_kernel, n_tiles),
        grid=(batch, n_tiles),
        in_specs=[pl.BlockSpec((tile, d), lambda b, j: (tile_row(b, j), 0)),
                  pl.BlockSpec((HALO_ROWS, d), lambda b, j: (
                      jnp.maximum(tile_row(b, j) * halo_per_tile - 1, 0), 0)),
                  pl.BlockSpec((HALO_ROWS, d), lambda b, j: (
                      jnp.minimum((tile_row(b, j) + 1) * halo_per_tile, last_halo), 0)),
                  _resident((d, 4 * gw), l), _resident((HEADS, HEAD_DIM), l),
                  _resident((CONV_WIDTH, gw), l), _resident((1, gw), l),
                  _resident((n_diag, MXU_WIDTH, MXU_WIDTH), l),
                  _resident((n_diag, MXU_WIDTH, MXU_WIDTH), l),
                  _resident((n_diag, MXU_WIDTH, MXU_WIDTH), l),
                  _resident((gw, GATE_LANES), l), _resident((gw, GATE_LANES), l),
                  _resident((1, GATE_LANES), l)],
        out_specs=[pl.BlockSpec((chunks, gw, CHUNK), lambda b, j: (tile_row(b, j), 0, 0))]
                  + _shared_specs(tile_row, tile),
        out_shape=[jax.ShapeDtypeStruct((tokens // CHUNK, gw, CHUNK), F32)] + shared_shapes,
        scratch_shapes=_state_scratch(),
        compiler_params=_compiler_params(2),
        name=f"mix_bwd_{l}",
    )(hn, hn, hn, win, sgun, *mw)


def _mix_fwd(h, hbt, shared, l, batch, sguw, sgub, mhn, skip, wout,
             g2, wg, wu, wd, gf, final_norm):
    tokens, d = h.shape
    hidden = wg.shape[-1]
    n_tiles = tokens // batch // MIX_TILE
    tile_row = _tile_row(n_tiles, False)
    gw = GROUP_WIDTH
    chunks = MIX_TILE // CHUNK
    rows2 = lambda b, j: (tile_row(b, j), 0)
    return pl.pallas_call(
        functools.partial(_mix_fwd_kernel, final_norm),
        grid=(batch, n_tiles),
        in_specs=[pl.BlockSpec((MIX_TILE, d), rows2),
                  pl.BlockSpec((chunks, gw, CHUNK), lambda b, j: (tile_row(b, j), 0, 0))]
                 + _shared_specs(tile_row, MIX_TILE)
                 + [_resident((HEADS, CHUNK, CHUNK), l),
                    _resident((HEADS, CHUNK, HEAD_DIM), l), _resident((HEADS, HEAD_DIM, CHUNK), l),
                    _resident((1, gw), l), _resident((2 * gw, d), l),
                    _resident((1, d), l), _resident((d, hidden), l), _resident((d, hidden), l),
                    _resident((hidden, d), l), _resident((1, d))],
        out_specs=pl.BlockSpec((MIX_TILE, d), rows2),
        out_shape=jax.ShapeDtypeStruct((tokens, d), F32),
        scratch_shapes=_state_scratch() + [pltpu.VMEM((chunks, gw, CHUNK), F32),
                                           pltpu.VMEM((MIX_TILE, 2 * gw), BF16)],
        compiler_params=_compiler_params(2),
        name=f"mix_fwd_{l}",
    )(h, hbt, *shared, sguw, sgub, mhn, skip, wout, g2, wg, wu, wd, gf)


def _diag_blocks(w, transpose=False, dtype=BF16):
    depth, groups, bi, bo = w.shape
    rows = w.reshape(depth, groups * bi // MXU_WIDTH, MXU_WIDTH, bo)
    idx = jnp.arange(MXU_WIDTH)
    tiled = jnp.einsum('dnro,oc->dnrc', rows, (idx[None, :] % bo == jnp.arange(bo)[:, None]).astype(F32),
                       precision=lax.Precision.HIGHEST)
    dense = jnp.where(idx[:, None] // bi == idx[None, :] // bo, tiled, 0.0)
    if transpose:
        dense = jnp.swapaxes(dense, -1, -2)
    return dense.astype(dtype)


def _pack_gates(w_fwd, b_fwd, w_bwd, b_bwd):
    depth, rows, n = w_fwd.shape
    w = jnp.concatenate([w_fwd, w_bwd, jnp.zeros((depth, rows, GATE_LANES - 2 * n), F32)], axis=-1)
    b = jnp.concatenate([b_fwd, b_bwd, jnp.zeros((depth, GATE_LANES - 2 * n), F32)], axis=-1)
    return w, b.reshape(depth, 1, GATE_LANES)


def _fold_gate_weights(w_q, w_k, w_v, wgate):
    depth = wgate.shape[0]
    gw = GROUP_WIDTH

    def folded(w, rows):
        blocks = _diag_blocks(w, dtype=F32)
        wg = wgate[:, rows:rows + gw].reshape(depth, gw // MXU_WIDTH, MXU_WIDTH, GATE_LANES)
        return jnp.einsum('dnrc,dnck->dnrk', blocks, wg,
                          precision=lax.Precision.HIGHEST).reshape(depth, gw, GATE_LANES)

    wgx = folded(w_q, 0) + folded(w_k, gw) * (HEAD_DIM ** -0.5)
    return wgx.astype(BF16), folded(w_v, 2 * gw).astype(BF16)


def kernel(x, ffn1_norm, ffn1_w_gate, ffn1_w_up, ffn1_w_down, mix_norm, w_in, sgu_norm, sgu_w, sgu_b, conv_w, conv_b, w_q, w_k, w_v, gate_w_fwd, gate_b_fwd, gate_w_bwd, gate_b_bwd, mh_norm, mlstm_skip, w_out, ffn2_norm, ffn2_w_gate, ffn2_w_up, ffn2_w_down, final_norm):
    batch, seq, d = x.shape
    depth = w_in.shape[0]
    assert seq % MIX_TILE == 0 and seq % MIX_BWD_TILE == 0 and (batch * seq) % FFN_TILE == 0
    assert sgu_w.shape[1:] == (HEADS, CHUNK, CHUNK) and mh_norm.shape[1:] == (HEADS, HEAD_DIM)
    assert gate_w_fwd.shape[1:] == (3 * GROUP_WIDTH, 2 * HEADS)

    bf = lambda w: w.astype(BF16)
    row3 = lambda g: g.reshape(depth, 1, -1)
    f1 = (row3(ffn1_norm), bf(ffn1_w_gate), bf(ffn1_w_up), bf(ffn1_w_down))
    f2 = (row3(ffn2_norm), bf(ffn2_w_gate), bf(ffn2_w_up), bf(ffn2_w_down))
    wgate, bgate = _pack_gates(gate_w_fwd, gate_b_fwd, gate_w_bwd, gate_b_bwd)
    convb = row3(conv_b)
    wgx, wgm = _fold_gate_weights(w_q, w_k, w_v, wgate)
    mw = (conv_w, convb, _diag_blocks(w_k), _diag_blocks(w_q, transpose=True),
          _diag_blocks(w_v, transpose=True), wgx, wgm, bgate)
    sgub = jnp.broadcast_to(sgu_b[..., None], sgu_b.shape + (HEAD_DIM,))
    mhn = jnp.broadcast_to(mh_norm[..., None], mh_norm.shape + (CHUNK,))
    gfin = final_norm.reshape(1, d)

    xt = x.reshape(batch * seq, d)
    for l in range(depth):
        h, hn = _ffn_in(xt, l, *f1, row3(mix_norm))
        hbt, *shared = _mix_bwd(hn, l, batch, bf(w_in), sgu_norm, mw)
        xt = _mix_fwd(h, hbt, shared, l, batch, bf(sgu_w), sgub, mhn,
                      row3(mlstm_skip), bf(w_out), *f2, gfin, final_norm=(l == depth - 1))
    return xt.reshape(batch, seq, d)
```

```python
import functools

import jax
import jax.numpy as jnp
from jax import lax
from jax.experimental import pallas as pl
from jax.experimental.pallas import tpu as pltpu

F32 = jnp.float32
BF16 = jnp.bfloat16

EPS = 1e-6
HEADS = 4
HEAD_DIM = 128
CHUNK = 128
GROUP_WIDTH = HEADS * HEAD_DIM
CONV_WIDTH = 5
QKV_BLOCK = 4
MXU_WIDTH = 256
GATE_LANES = 128
GATE_ROWS = 4 * HEADS
IG_LANE = {False: 0, True: 2 * HEADS}
B_LANE = {False: HEADS, True: 3 * HEADS}
STATE_ROWS = HEAD_DIM + 32
HALO_ROWS = 16

FFN_TILE = 1024
MIX_TILE = 512
MIX_BWD_TILE = 1024
FFN_HIDDEN_CHUNK = 256
VMEM_LIMIT_BYTES = 56 * 1024 * 1024

_NT = (((1,), (1,)), ((), ()))
LOG2_E = 1.4426950408889634


def _rms_norm(x, g):
    return x * lax.rsqrt(jnp.mean(x * x, axis=-1, keepdims=True) + EPS) * g


def _layer_norm(x, g):
    mu = jnp.mean(x, axis=-1, keepdims=True)
    xc = x - mu
    var = jnp.mean(xc * xc, axis=-1, keepdims=True)
    return xc * lax.rsqrt(var + EPS) * g


def _gelu_tanh(x):
    c2 = 2.0 * (2.0 / jnp.pi) ** 0.5 * LOG2_E
    return x / (1.0 + jnp.exp2(x * (x * x * (-0.044715 * c2) - c2)))


def _log_sigmoid(x):
    return jnp.minimum(x, 0.0) - jnp.log1p(jnp.exp(-jnp.abs(x)))


def _swiglu(xn, wg_ref, wu_ref, wd_ref):
    hidden = wg_ref.shape[1]
    acc = None
    for lo in range(0, hidden, FFN_HIDDEN_CHUNK):
        hi = min(lo + FFN_HIDDEN_CHUNK, hidden)
        gate = jnp.dot(xn, wg_ref[:, lo:hi], preferred_element_type=F32)
        up = jnp.dot(xn, wu_ref[:, lo:hi], preferred_element_type=F32)
        act = (jax.nn.silu(gate) * up).astype(BF16)
        part = jnp.dot(act, wd_ref[lo:hi, :], preferred_element_type=F32)
        acc = part if acc is None else acc + part
    return acc


def _short_conv(rows, convw_ref, convb_ref):
    n = rows.shape[0]
    conv = None
    for tap in range(CONV_WIDTH):
        off = tap - CONV_WIDTH // 2
        shifted = rows if off == 0 else pltpu.roll(rows, (n - off) % n, axis=0)
        term = shifted * convw_ref[tap:tap + 1, :]
        conv = term if conv is None else conv + term
    return jax.nn.silu(conv + convb_ref[...])


def _ffn_in_kernel(x_ref, g1_ref, wg_ref, wu_ref, wd_ref, g2_ref, h_ref, hn_ref):
    x = x_ref[...]
    xn = _rms_norm(x, g1_ref[...]).astype(BF16)
    h = x + 0.5 * _swiglu(xn, wg_ref, wu_ref, wd_ref)
    h_ref[...] = h
    hn_ref[...] = _rms_norm(h, g2_ref[...]).astype(BF16)


def _block_diag_dot(x, bd_ref):
    parts = [jnp.dot(x[:, i * MXU_WIDTH:(i + 1) * MXU_WIDTH], bd_ref[i], preferred_element_type=F32)
             for i in range(bd_ref.shape[0])]
    return jnp.concatenate(parts, axis=1)


def _mlstm_prepare(first, last, hn_ref, hprev_ref, hnext_ref, win_ref, sgun_ref, convw_ref, convb_ref,
                   bdk_ref, bdqt_ref, bdvt_ref, wgx_ref, wgm_ref, bgate_ref,
                   elem_ref, vn_ref, qt_ref, k_ref, vt_ref, ga_ref, gr_ref):
    tile = hn_ref.shape[0]
    gw = GROUP_WIDTH
    r = HALO_ROWS
    hn = hn_ref[...]

    def z_group(rows, g):
        return jnp.dot(rows, win_ref[:, g * gw:(g + 1) * gw], preferred_element_type=F32)

    xm_ext = z_group(jnp.concatenate([hprev_ref[...], hn, hnext_ref[...]], axis=0), 2)
    xm = xm_ext[r:r + tile]
    ext = jnp.concatenate([jnp.where(first, 0.0, xm_ext[:r]), xm,
                           jnp.where(last, 0.0, xm_ext[r + tile:])], axis=0)
    xc = _short_conv(ext, convw_ref, convb_ref)[r:r + tile]
    elem_ref[:, 2 * gw:] = xc

    v = z_group(hn, 1)
    for hd in range(HEADS):
        cols = slice(hd * HEAD_DIM, (hd + 1) * HEAD_DIM)
        vn_ref[:, cols] = _layer_norm(_gelu_tanh(v[:, cols]), sgun_ref[hd:hd + 1, :]).astype(BF16)
    elem_ref[:, :gw] = _gelu_tanh(z_group(hn, 0))
    elem_ref[:, gw:2 * gw] = jax.nn.sigmoid(z_group(hn, 3))

    xc_b = xc.astype(BF16)
    xm_b = xm.astype(BF16)
    k_ref[...] = (_block_diag_dot(xc_b, bdk_ref) * (HEAD_DIM ** -0.5)).astype(BF16)
    gates = (jnp.dot(xc_b, wgx_ref[...], preferred_element_type=F32)
             + jnp.dot(xm_b, wgm_ref[...], preferred_element_type=F32)
             + bgate_ref[...])
    n_chunks = tile // CHUNK

    for i in range(bdvt_ref.shape[0]):
        blk = slice(i * MXU_WIDTH, (i + 1) * MXU_WIDTH)
        qt = lax.dot_general(bdqt_ref[i], xc_b[:, blk], _NT, preferred_element_type=F32).astype(BF16)
        vt = lax.dot_general(bdvt_ref[i], xm_b[:, blk], _NT, preferred_element_type=F32)
        for c in range(n_chunks):
            qt_ref[c, blk, :] = qt[:, c * CHUNK:(c + 1) * CHUNK]
            vt_ref[c, blk, :] = vt[:, c * CHUNK:(c + 1) * CHUNK]

    lane = lax.broadcasted_iota(jnp.int32, gates.shape, 1)
    is_forget = (((lane >= B_LANE[False]) & (lane < B_LANE[False] + HEADS))
                 | ((lane >= B_LANE[True]) & (lane < B_LANE[True] + HEADS)))
    mixed = jnp.where(is_forget, _log_sigmoid(gates), gates) * LOG2_E
    gt = jnp.concatenate([mixed[c * CHUNK:(c + 1) * CHUNK].T[:GATE_ROWS] for c in range(n_chunks)], axis=0)
    n_rows = n_chunks * GATE_ROWS
    hi = gt.astype(BF16)
    rem = gt - hi.astype(F32)
    mid = rem.astype(BF16)
    pieces = jnp.concatenate([hi, mid, (rem - mid.astype(F32)).astype(BF16)], axis=0)
    src = lax.broadcasted_iota(jnp.int32, (CHUNK, CHUNK), 0)
    dst = lax.broadcasted_iota(jnp.int32, (CHUNK, CHUNK), 1)

    def prefix_sum(includes):
        parts = jnp.dot(pieces, includes.astype(BF16), preferred_element_type=F32)
        return parts[:n_rows] + parts[n_rows:2 * n_rows] + parts[2 * n_rows:]

    row = lax.broadcasted_iota(jnp.int32, gt.shape, 0) % GATE_ROWS
    gr = jnp.where((row >= B_LANE[False]) & (row < B_LANE[False] + HEADS), prefix_sum(src <= dst),
                   jnp.where((row >= B_LANE[True]) & (row < B_LANE[True] + HEADS), prefix_sum(src >= dst), gt))
    gr_ref[...] = gr
    a_rows = gr - pltpu.roll(gr, n_rows - HEADS, axis=0)
    pad = jnp.zeros((GATE_LANES - GATE_ROWS, CHUNK), F32)
    for c in range(n_chunks):
        a_c = jnp.concatenate([a_rows[c * GATE_ROWS:(c + 1) * GATE_ROWS], pad], axis=0)
        ga_ref[c * CHUNK:(c + 1) * CHUNK, :] = a_c.T


def _mlstm_direction(reverse, n_chunks, qt_ref, k_ref, vt_ref, ga_ref, gr_ref, state_s, m_s, out_ref):
    key = lax.broadcasted_iota(jnp.int32, (CHUNK, CHUNK), 0)
    qry = lax.broadcasted_iota(jnp.int32, (CHUNK, CHUNK), 1)
    visible = (key >= qry) if reverse else (key <= qry)
    first_row = lax.broadcasted_iota(jnp.int32, (STATE_ROWS - HEAD_DIM, CHUNK), 0) == 0
    edge = 0 if reverse else CHUNK - 1
    ig0, b0 = IG_LANE[reverse], B_LANE[reverse]

    order = [(n_chunks - 1 - i) if reverse else i for i in range(n_chunks)]
    head_cols = [slice(h * HEAD_DIM, (h + 1) * HEAD_DIM) for h in range(HEADS)]
    chunk_rows = [slice(c * CHUNK, (c + 1) * CHUNK) for c in range(n_chunks)]
    gr = [gr_ref[c * GATE_ROWS:(c + 1) * GATE_ROWS, :] for c in range(n_chunks)]
    b_rows = {(c, h): gr[c][b0 + h:b0 + h + 1, :] for c in order for h in range(HEADS)}

    incr = {}
    for c in order:
        for h in range(HEADS):
            b_row = b_rows[c, h]
            ig_row = gr[c][ig0 + h:ig0 + h + 1, :]
            b_edge = jnp.broadcast_to(b_row[:, edge:edge + 1], b_row.shape)
            dec = (b_edge - b_row) + ig_row
            dec_max = jnp.broadcast_to(jnp.max(dec, axis=1, keepdims=True), dec.shape)
            wk = jnp.exp2(dec - dec_max)
            lhs = jnp.concatenate([vt_ref[c, head_cols[h], :] * wk, jnp.where(first_row, wk, 0.0)],
                                  axis=0).astype(BF16)
            update = jnp.dot(lhs, k_ref[chunk_rows[c], head_cols[h]], preferred_element_type=F32)
            incr[c, h] = (b_edge, dec_max, update)

    states = [state_s[h] for h in range(HEADS)]
    ms = [m_s[h:h + 1, :] for h in range(HEADS)]
    partial = {}
    for c in order:
        ga = ga_ref[chunk_rows[c], :]
        for h in range(HEADS):
            b_row = b_rows[c, h]
            a_col = ga[:, ig0 + h:ig0 + h + 1]
            state, m_prev = states[h], ms[h]
            d = jnp.where(visible, b_row + a_col, -jnp.inf)
            m_loc = jnp.max(d, axis=0, keepdims=True)
            lhs = jnp.concatenate([k_ref[chunk_rows[c], head_cols[h]], state.astype(BF16)], axis=0)
            both = jnp.dot(lhs, qt_ref[c, head_cols[h], :], preferred_element_type=F32)
            st = both[:CHUNK] * jnp.exp2(d - m_loc)
            carried = both[CHUNK:]
            inter = b_row + m_prev
            m_j = jnp.maximum(inter, m_loc)
            w_inter = jnp.exp2(inter - m_j)
            w_loc = jnp.exp2(m_loc - m_j)
            nq = (w_inter * carried[HEAD_DIM:HEAD_DIM + 1]
                  + w_loc * jnp.sum(st, axis=0, keepdims=True))
            inv = 1.0 / jnp.maximum(jnp.abs(nq), jnp.exp2(-m_j))
            partial[c, h] = (st.astype(BF16), w_inter * inv, w_loc * inv, carried[:HEAD_DIM])

            b_edge, dec_max, update = incr[c, h]
            m_new = jnp.maximum(b_edge + m_prev, dec_max)
            states[h] = (jnp.exp2(b_edge + m_prev - m_new) * state
                         + jnp.exp2(dec_max - m_new) * update)
            ms[h] = m_new
    for h in range(HEADS):
        state_s[h] = states[h]
        m_s[h:h + 1, :] = ms[h]

    for c in order:
        for h in range(HEADS):
            st_b, s_inter, s_loc, carried = partial[c, h]
            num_loc = jnp.dot(vt_ref[c, head_cols[h], :].astype(BF16), st_b, preferred_element_type=F32)
            out_ref[c, head_cols[h], :] = s_inter * carried + s_loc * num_loc


def _reset_state(state_s, m_s):
    state_s[...] = jnp.zeros_like(state_s)
    m_s[...] = jnp.zeros_like(m_s)


def _mix_bwd_kernel(n_tiles, hn_ref, hprev_ref, hnext_ref, win_ref, sgun_ref, convw_ref, convb_ref, bdk_ref,
                    bdqt_ref, bdvt_ref, wgx_ref, wgm_ref, bgate_ref,
                    hbt_ref, elem_ref, vn_ref, qt_ref, k_ref, vt_ref, ga_ref, gr_ref,
                    state_s, m_s):
    j = pl.program_id(1)
    seq_tile = n_tiles - 1 - j

    @pl.when(j == 0)
    def _():
        _reset_state(state_s, m_s)

    _mlstm_prepare(seq_tile == 0, seq_tile == n_tiles - 1, hn_ref, hprev_ref, hnext_ref, win_ref, sgun_ref,
                   convw_ref, convb_ref, bdk_ref, bdqt_ref, bdvt_ref, wgx_ref, wgm_ref, bgate_ref,
                   elem_ref, vn_ref, qt_ref, k_ref, vt_ref, ga_ref, gr_ref)
    _mlstm_direction(True, hn_ref.shape[0] // CHUNK, qt_ref, k_ref, vt_ref, ga_ref, gr_ref,
                     state_s, m_s, hbt_ref)


def _mix_fwd_kernel(final_norm, h_ref, hbt_ref, elem_ref, vn_ref, qt_ref, k_ref, vt_ref,
                    ga_ref, gr_ref, sguw_ref, sgub_ref, mhn_ref, skip_ref, wout_ref,
                    g2_ref, wg_ref, wu_ref, wd_ref, gf_ref, o_ref,
                    state_s, m_s, hft_s, y_s):
    tile = h_ref.shape[0]

    @pl.when(pl.program_id(1) == 0)
    def _():
        _reset_state(state_s, m_s)

    _mlstm_direction(False, tile // CHUNK, qt_ref, k_ref, vt_ref, ga_ref, gr_ref, state_s, m_s, hft_s)

    for h in range(HEADS):
        cols = slice(h * HEAD_DIM, (h + 1) * HEAD_DIM)
        for c in range(tile // CHUNK):
            rows = slice(c * CHUNK, (c + 1) * CHUNK)
            mixed = jnp.dot(sguw_ref[h], vn_ref[rows, cols], preferred_element_type=F32) + sgub_ref[h]
            y_s[rows, cols] = (elem_ref[rows, cols] * mixed).astype(BF16)

    n_chunks = tile // CHUNK
    for c in range(n_chunks):
        rows = slice(c * CHUNK, (c + 1) * CHUNK)
        for h in range(HEADS):
            cols = slice(h * HEAD_DIM, (h + 1) * HEAD_DIM)
            sg_cols = slice(GROUP_WIDTH + h * HEAD_DIM, GROUP_WIDTH + (h + 1) * HEAD_DIM)
            xc_cols = slice(2 * GROUP_WIDTH + h * HEAD_DIM, 2 * GROUP_WIDTH + (h + 1) * HEAD_DIM)
            ht = hft_s[c, cols, :] + hbt_ref[c, cols, :]
            mu = jnp.mean(ht, axis=0, keepdims=True)
            hc = ht - mu
            var = jnp.mean(hc * hc, axis=0, keepdims=True)
            hn = (hc * lax.rsqrt(var + EPS) * mhn_ref[h]).T
            ym = (hn + skip_ref[:, cols] * elem_ref[rows, xc_cols]) * elem_ref[rows, sg_cols]
            y_s[rows, sg_cols] = ym.astype(BF16)

    h = h_ref[...] + jnp.dot(y_s[...], wout_ref[...], preferred_element_type=F32)

    hn = _rms_norm(h, g2_ref[...]).astype(BF16)
    x = h + 0.5 * _swiglu(hn, wg_ref, wu_ref, wd_ref)
    if final_norm:
        x = _rms_norm(x, gf_ref[...])
    o_ref[...] = x


def _compiler_params(grid_rank):
    return pltpu.CompilerParams(dimension_semantics=("arbitrary",) * grid_rank,
                                vmem_limit_bytes=VMEM_LIMIT_BYTES)


def _resident(shape, layer=None):
    nd = len(shape)
    if layer is None:
        return pl.BlockSpec(shape, lambda *_: (0,) * nd, pipeline_mode=pl.Buffered(1))
    return pl.BlockSpec((None,) + shape, lambda *_: (layer,) + (0,) * nd,
                        pipeline_mode=pl.Buffered(1))


def _ffn_in(x, l, g1, wg, wu, wd, g2):
    tokens, d = x.shape
    hidden = wg.shape[-1]
    row = lambda i: (i, 0)
    return pl.pallas_call(
        _ffn_in_kernel,
        grid=(tokens // FFN_TILE,),
        in_specs=[pl.BlockSpec((FFN_TILE, d), row),
                  _resident((1, d), l), _resident((d, hidden), l), _resident((d, hidden), l),
                  _resident((hidden, d), l), _resident((1, d), l)],
        out_specs=[pl.BlockSpec((FFN_TILE, d), row), pl.BlockSpec((FFN_TILE, d), row)],
        out_shape=[jax.ShapeDtypeStruct((tokens, d), F32), jax.ShapeDtypeStruct((tokens, d), BF16)],
        compiler_params=_compiler_params(1),
        name=f"ffn_in_{l}",
    )(x, g1, wg, wu, wd, g2)


def _state_scratch():
    return [pltpu.VMEM((HEADS, STATE_ROWS, HEAD_DIM), F32),
            pltpu.VMEM((8, 128), F32)]


def _tile_row(n_tiles, reverse):
    return lambda b, j: b * n_tiles + ((n_tiles - 1 - j) if reverse else j)


def _shared_specs(tile_row, tile):
    gw = GROUP_WIDTH
    chunks = tile // CHUNK
    rows2 = lambda b, j: (tile_row(b, j), 0)
    rows3 = lambda b, j: (tile_row(b, j), 0, 0)
    return [pl.BlockSpec((tile, 3 * gw), rows2),
            pl.BlockSpec((tile, gw), rows2),
            pl.BlockSpec((chunks, gw, CHUNK), rows3),
            pl.BlockSpec((tile, gw), rows2),
            pl.BlockSpec((chunks, gw, CHUNK), rows3),
            pl.BlockSpec((tile, GATE_LANES), rows2),
            pl.BlockSpec((chunks * GATE_ROWS, CHUNK), rows2)]


def _mix_bwd(hn, l, batch, win, sgun, mw):
    tokens, d = hn.shape
    tile = MIX_BWD_TILE
    n_tiles = tokens // batch // tile
    tile_row = _tile_row(n_tiles, True)
    gw = GROUP_WIDTH
    halo_per_tile = tile // HALO_ROWS
    last_halo = tokens // HALO_ROWS - 1
    n_diag = gw // MXU_WIDTH
    chunks = tile // CHUNK
    shared_shapes = [jax.ShapeDtypeStruct((tokens, 3 * gw), F32), jax.ShapeDtypeStruct((tokens, gw), BF16),
                     jax.ShapeDtypeStruct((tokens // CHUNK, gw, CHUNK), BF16),
                     jax.ShapeDtypeStruct((tokens, gw), BF16),
                     jax.ShapeDtypeStruct((tokens // CHUNK, gw, CHUNK), F32),
                     jax.ShapeDtypeStruct((tokens, GATE_LANES), F32),
                     jax.ShapeDtypeStruct((tokens // CHUNK * GATE_ROWS, CHUNK), F32)]
    return pl.pallas_call(
        functools.partial(_mix_bwd_kernel, n_tiles),
        grid=(batch, n_tiles),
        in_specs=[pl.BlockSpec((tile, d), lambda b, j: (tile_row(b, j), 0)),
                  pl.BlockSpec((HALO_ROWS, d), lambda b, j: (
                      jnp.maximum(tile_row(b, j) * halo_per_tile - 1, 0), 0)),
                  pl.BlockSpec((HALO_ROWS, d), lambda b, j: (
                      jnp.minimum((tile_row(b, j) + 1) * halo_per_tile, last_halo), 0)),
                  _resident((d, 4 * gw), l), _resident((HEADS, HEAD_DIM), l),
                  _resident((CONV_WIDTH, gw), l), _resident((1, gw), l),
                  _resident((n_diag, MXU_WIDTH, MXU_WIDTH), l),
                  _resident((n_diag, MXU_WIDTH, MXU_WIDTH), l),
                  _resident((n_diag, MXU_WIDTH, MXU_WIDTH), l),
                  _resident((gw, GATE_LANES), l), _resident((gw, GATE_LANES), l),
                  _resident((1, GATE_LANES), l)],
        out_specs=[pl.BlockSpec((chunks, gw, CHUNK), lambda b, j: (tile_row(b, j), 0, 0))]
                  + _shared_specs(tile_row, tile),
        out_shape=[jax.ShapeDtypeStruct((tokens // CHUNK, gw, CHUNK), F32)] + shared_shapes,
        scratch_shapes=_state_scratch(),
        compiler_params=_compiler_params(2),
        name=f"mix_bwd_{l}",
    )(hn, hn, hn, win, sgun, *mw)


def _mix_fwd(h, hbt, shared, l, batch, sguw, sgub, mhn, skip, wout,
             g2, wg, wu, wd, gf, final_norm):
    tokens, d = h.shape
    hidden = wg.shape[-1]
    n_tiles = tokens // batch // MIX_TILE
    tile_row = _tile_row(n_tiles, False)
    gw = GROUP_WIDTH
    chunks = MIX_TILE // CHUNK
    rows2 = lambda b, j: (tile_row(b, j), 0)
    return pl.pallas_call(
        functools.partial(_mix_fwd_kernel, final_norm),
        grid=(batch, n_tiles),
        in_specs=[pl.BlockSpec((MIX_TILE, d), rows2),
                  pl.BlockSpec((chunks, gw, CHUNK), lambda b, j: (tile_row(b, j), 0, 0))]
                 + _shared_specs(tile_row, MIX_TILE)
                 + [_resident((HEADS, CHUNK, CHUNK), l),
                    _resident((HEADS, CHUNK, HEAD_DIM), l), _resident((HEADS, HEAD_DIM, CHUNK), l),
                    _resident((1, gw), l), _resident((2 * gw, d), l),
                    _resident((1, d), l), _resident((d, hidden), l), _resident((d, hidden), l),
                    _resident((hidden, d), l), _resident((1, d))],
        out_specs=pl.BlockSpec((MIX_TILE, d), rows2),
        out_shape=jax.ShapeDtypeStruct((tokens, d), F32),
        scratch_shapes=_state_scratch() + [pltpu.VMEM((chunks, gw, CHUNK), F32),
                                           pltpu.VMEM((MIX_TILE, 2 * gw), BF16)],
        compiler_params=_compiler_params(2),
        name=f"mix_fwd_{l}",
    )(h, hbt, *shared, sguw, sgub, mhn, skip, wout, g2, wg, wu, wd, gf)


def _diag_blocks(w, transpose=False, dtype=BF16):
    depth, groups, bi, bo = w.shape
    rows = w.reshape(depth, groups * bi // MXU_WIDTH, MXU_WIDTH, bo)
    idx = jnp.arange(MXU_WIDTH)
    tiled = jnp.einsum('dnro,oc->dnrc', rows, (idx[None, :] % bo == jnp.arange(bo)[:, None]).astype(F32),
                       precision=lax.Precision.HIGHEST)
    dense = jnp.where(idx[:, None] // bi == idx[None, :] // bo, tiled, 0.0)
    if transpose:
        dense = jnp.swapaxes(dense, -1, -2)
    return dense.astype(dtype)


def _pack_gates(w_fwd, b_fwd, w_bwd, b_bwd):
    depth, rows, n = w_fwd.shape
    w = jnp.concatenate([w_fwd, w_bwd, jnp.zeros((depth, rows, GATE_LANES - 2 * n), F32)], axis=-1)
    b = jnp.concatenate([b_fwd, b_bwd, jnp.zeros((depth, GATE_LANES - 2 * n), F32)], axis=-1)
    return w, b.reshape(depth, 1, GATE_LANES)


def _fold_gate_weights(w_q, w_k, w_v, wgate):
    depth = wgate.shape[0]
    gw = GROUP_WIDTH

    def folded(w, rows):
        blocks = _diag_blocks(w, dtype=F32)
        wg = wgate[:, rows:rows + gw].reshape(depth, gw // MXU_WIDTH, MXU_WIDTH, GATE_LANES)
        return jnp.einsum('dnrc,dnck->dnrk', blocks, wg,
                          precision=lax.Precision.HIGHEST).reshape(depth, gw, GATE_LANES)

    wgx = folded(w_q, 0) + folded(w_k, gw) * (HEAD_DIM ** -0.5)
    return wgx.astype(BF16), folded(w_v, 2 * gw).astype(BF16)


def kernel(x, ffn1_norm, ffn1_w_gate, ffn1_w_up, ffn1_w_down, mix_norm, w_in, sgu_norm, sgu_w, sgu_b, conv_w, conv_b, w_q, w_k, w_v, gate_w_fwd, gate_b_fwd, gate_w_bwd, gate_b_bwd, mh_norm, mlstm_skip, w_out, ffn2_norm, ffn2_w_gate, ffn2_w_up, ffn2_w_down, final_norm):
    batch, seq, d = x.shape
    depth = w_in.shape[0]
    assert seq % MIX_TILE == 0 and seq % MIX_BWD_TILE == 0 and (batch * seq) % FFN_TILE == 0
    assert sgu_w.shape[1:] == (HEADS, CHUNK, CHUNK) and mh_norm.shape[1:] == (HEADS, HEAD_DIM)
    assert gate_w_fwd.shape[1:] == (3 * GROUP_WIDTH, 2 * HEADS)

    bf = lambda w: w.astype(BF16)
    row3 = lambda g: g.reshape(depth, 1, -1)
    f1 = (row3(ffn1_norm), bf(ffn1_w_gate), bf(ffn1_w_up), bf(ffn1_w_down))
    f2 = (row3(ffn2_norm), bf(ffn2_w_gate), bf(ffn2_w_up), bf(ffn2_w_down))
    wgate, bgate = _pack_gates(gate_w_fwd, gate_b_fwd, gate_w_bwd, gate_b_bwd)
    convb = row3(conv_b)
    wgx, wgm = _fold_gate_weights(w_q, w_k, w_v, wgate)
    mw = (conv_w, convb, _diag_blocks(w_k), _diag_blocks(w_q, transpose=True),
          _diag_blocks(w_v, transpose=True), wgx, wgm, bgate)
    sgub = jnp.broadcast_to(sgu_b[..., None], sgu_b.shape + (HEAD_DIM,))
    mhn = jnp.broadcast_to(mh_norm[..., None], mh_norm.shape + (CHUNK,))
    gfin = final_norm.reshape(1, d)

    xt = x.reshape(batch * seq, d)
    for l in range(depth):
        h, hn = _ffn_in(xt, l, *f1, row3(mix_norm))
        hbt, *shared = _mix_bwd(hn, l, batch, bf(w_in), sgu_norm, mw)
        xt = _mix_fwd(h, hbt, shared, l, batch, bf(sgu_w), sgub, mhn,
                      row3(mlstm_skip), bf(w_out), *f2, gfin, final_norm=(l == depth - 1))
    return xt.reshape(batch, seq, d)
```

```python
import functools

import jax
import jax.numpy as jnp
from jax import lax
from jax.experimental import pallas as pl
from jax.experimental.pallas import tpu as pltpu

F32 = jnp.float32
BF16 = jnp.bfloat16

EPS = 1e-6
HEADS = 4
HEAD_DIM = 128
CHUNK = 128
GROUP_WIDTH = HEADS * HEAD_DIM
CONV_WIDTH = 5
QKV_BLOCK = 4
MXU_WIDTH = 256
GATE_LANES = 128
GATE_ROWS = 4 * HEADS
IG_LANE = {False: 0, True: 2 * HEADS}
B_LANE = {False: HEADS, True: 3 * HEADS}
STATE_ROWS = HEAD_DIM + 32
HALO_ROWS = 16

FFN_TILE = 1024
MIX_TILE = 512
MIX_BWD_TILE = 1024
FFN_HIDDEN_CHUNK = 256
VMEM_LIMIT_BYTES = 56 * 1024 * 1024

_NT = (((1,), (1,)), ((), ()))
LOG2_E = 1.4426950408889634


def _rms_norm(x, g):
    return x * lax.rsqrt(jnp.mean(x * x, axis=-1, keepdims=True) + EPS) * g


def _layer_norm(x, g):
    mu = jnp.mean(x, axis=-1, keepdims=True)
    xc = x - mu
    var = jnp.mean(xc * xc, axis=-1, keepdims=True)
    return xc * lax.rsqrt(var + EPS) * g


def _gelu_tanh(x):
    c2 = 2.0 * (2.0 / jnp.pi) ** 0.5 * LOG2_E
    return x / (1.0 + jnp.exp2(x * (x * x * (-0.044715 * c2) - c2)))


def _log_sigmoid(x):
    return jnp.minimum(x, 0.0) - jnp.log1p(jnp.exp(-jnp.abs(x)))


def _swiglu(xn, wg_ref, wu_ref, wd_ref):
    hidden = wg_ref.shape[1]
    acc = None
    for lo in range(0, hidden, FFN_HIDDEN_CHUNK):
        hi = min(lo + FFN_HIDDEN_CHUNK, hidden)
        gate = jnp.dot(xn, wg_ref[:, lo:hi], preferred_element_type=F32)
        up = jnp.dot(xn, wu_ref[:, lo:hi], preferred_element_type=F32)
        act = (jax.nn.silu(gate) * up).astype(BF16)
        part = jnp.dot(act, wd_ref[lo:hi, :], preferred_element_type=F32)
        acc = part if acc is None else acc + part
    return acc


def _short_conv(rows, convw_ref, convb_ref):
    n = rows.shape[0]
    conv = None
    for tap in range(CONV_WIDTH):
        off = tap - CONV_WIDTH // 2
        shifted = rows if off == 0 else pltpu.roll(rows, (n - off) % n, axis=0)
        term = shifted * convw_ref[tap:tap + 1, :]
        conv = term if conv is None else conv + term
    return jax.nn.silu(conv + convb_ref[...])


def _ffn_in_kernel(x_ref, g1_ref, wg_ref, wu_ref, wd_ref, g2_ref, h_ref, hn_ref):
    x = x_ref[...]
    xn = _rms_norm(x, g1_ref[...]).astype(BF16)
    h = x + 0.5 * _swiglu(xn, wg_ref, wu_ref, wd_ref)
    h_ref[...] = h
    hn_ref[...] = _rms_norm(h, g2_ref[...]).astype(BF16)


def _block_diag_dot(x, bd_ref):
    parts = [jnp.dot(x[:, i * MXU_WIDTH:(i + 1) * MXU_WIDTH], bd_ref[i], preferred_element_type=F32)
             for i in range(bd_ref.shape[0])]
    return jnp.concatenate(parts, axis=1)


def _mlstm_prepare(first, last, hn_ref, hprev_ref, hnext_ref, win_ref, sgun_ref, convw_ref, convb_ref,
                   bdk_ref, bdqt_ref, bdvt_ref, wgx_ref, wgm_ref, bgate_ref,
                   elem_ref, vn_ref, qt_ref, k_ref, vt_ref, ga_ref, gr_ref):
    tile = hn_ref.shape[0]
    gw = GROUP_WIDTH
    r = HALO_ROWS
    hn = hn_ref[...]

    def z_group(rows, g):
        return jnp.dot(rows, win_ref[:, g * gw:(g + 1) * gw], preferred_element_type=F32)

    xm_ext = z_group(jnp.concatenate([hprev_ref[...], hn, hnext_ref[...]], axis=0), 2)
    xm = xm_ext[r:r + tile]
    ext = jnp.concatenate([jnp.where(first, 0.0, xm_ext[:r]), xm,
                           jnp.where(last, 0.0, xm_ext[r + tile:])], axis=0)
    xc = _short_conv(ext, convw_ref, convb_ref)[r:r + tile]
    elem_ref[:, 2 * gw:] = xc

    v = z_group(hn, 1)
    for hd in range(HEADS):
        cols = slice(hd * HEAD_DIM, (hd + 1) * HEAD_DIM)
        vn_ref[:, cols] = _layer_norm(_gelu_tanh(v[:, cols]), sgun_ref[hd:hd + 1, :]).astype(BF16)
    elem_ref[:, :gw] = _gelu_tanh(z_group(hn, 0))
    elem_ref[:, gw:2 * gw] = jax.nn.sigmoid(z_group(hn, 3))

    xc_b = xc.astype(BF16)
    xm_b = xm.astype(BF16)
    k_ref[...] = (_block_diag_dot(xc_b, bdk_ref) * (HEAD_DIM ** -0.5)).astype(BF16)
    gates = (jnp.dot(xc_b, wgx_ref[...], preferred_element_type=F32)
             + jnp.dot(xm_b, wgm_ref[...], preferred_element_type=F32)
             + bgate_ref[...])
    n_chunks = tile // CHUNK

    for i in range(bdvt_ref.shape[0]):
        blk = slice(i * MXU_WIDTH, (i + 1) * MXU_WIDTH)
        qt = lax.dot_general(bdqt_ref[i], xc_b[:, blk], _NT, preferred_element_type=F32).astype(BF16)
        vt = lax.dot_general(bdvt_ref[i], xm_b[:, blk], _NT, preferred_element_type=F32)
        for c in range(n_chunks):
            qt_ref[c, blk, :] = qt[:, c * CHUNK:(c + 1) * CHUNK]
            vt_ref[c, blk, :] = vt[:, c * CHUNK:(c + 1) * CHUNK]

    lane = lax.broadcasted_iota(jnp.int32, gates.shape, 1)
    is_forget = (((lane >= B_LANE[False]) & (lane < B_LANE[False] + HEADS))
                 | ((lane >= B_LANE[True]) & (lane < B_LANE[True] + HEADS)))
    mixed = jnp.where(is_forget, _log_sigmoid(gates), gates) * LOG2_E
    gt = jnp.concatenate([mixed[c * CHUNK:(c + 1) * CHUNK].T[:GATE_ROWS] for c in range(n_chunks)], axis=0)
    n_rows = n_chunks * GATE_ROWS
    hi = gt.astype(BF16)
    rem = gt - hi.astype(F32)
    mid = rem.astype(BF16)
    pieces = jnp.concatenate([hi, mid, (rem - mid.astype(F32)).astype(BF16)], axis=0)
    src = lax.broadcasted_iota(jnp.int32, (CHUNK, CHUNK), 0)
    dst = lax.broadcasted_iota(jnp.int32, (CHUNK, CHUNK), 1)

    def prefix_sum(includes):
        parts = jnp.dot(pieces, includes.astype(BF16), preferred_element_type=F32)
        return parts[:n_rows] + parts[n_rows:2 * n_rows] + parts[2 * n_rows:]

    row = lax.broadcasted_iota(jnp.int32, gt.shape, 0) % GATE_ROWS
    gr = jnp.where((row >= B_LANE[False]) & (row < B_LANE[False] + HEADS), prefix_sum(src <= dst),
                   jnp.where((row >= B_LANE[True]) & (row < B_LANE[True] + HEADS), prefix_sum(src >= dst), gt))
    gr_ref[...] = gr
    a_rows = gr - pltpu.roll(gr, n_rows - HEADS, axis=0)
    pad = jnp.zeros((GATE_LANES - GATE_ROWS, CHUNK), F32)
    for c in range(n_chunks):
        a_c = jnp.concatenate([a_rows[c * GATE_ROWS:(c + 1) * GATE_ROWS], pad], axis=0)
        ga_ref[c * CHUNK:(c + 1) * CHUNK, :] = a_c.T


def _mlstm_direction(reverse, n_chunks, qt_ref, k_ref, vt_ref, ga_ref, gr_ref, state_s, m_s, out_ref):
    key = lax.broadcasted_iota(jnp.int32, (CHUNK, CHUNK), 0)
    qry = lax.broadcasted_iota(jnp.int32, (CHUNK, CHUNK), 1)
    visible = (key >= qry) if reverse else (key <= qry)
    first_row = lax.broadcasted_iota(jnp.int32, (STATE_ROWS - HEAD_DIM, CHUNK), 0) == 0
    edge = 0 if reverse else CHUNK - 1
    ig0, b0 = IG_LANE[reverse], B_LANE[reverse]

    order = [(n_chunks - 1 - i) if reverse else i for i in range(n_chunks)]
    head_cols = [slice(h * HEAD_DIM, (h + 1) * HEAD_DIM) for h in range(HEADS)]
    chunk_rows = [slice(c * CHUNK, (c + 1) * CHUNK) for c in range(n_chunks)]
    gr = [gr_ref[c * GATE_ROWS:(c + 1) * GATE_ROWS, :] for c in range(n_chunks)]
    b_rows = {(c, h): gr[c][b0 + h:b0 + h + 1, :] for c in order for h in range(HEADS)}

    incr = {}
    for c in order:
        for h in range(HEADS):
            b_row = b_rows[c, h]
            ig_row = gr[c][ig0 + h:ig0 + h + 1, :]
            b_edge = jnp.broadcast_to(b_row[:, edge:edge + 1], b_row.shape)
            dec = (b_edge - b_row) + ig_row
            dec_max = jnp.broadcast_to(jnp.max(dec, axis=1, keepdims=True), dec.shape)
            wk = jnp.exp2(dec - dec_max)
            lhs = jnp.concatenate([vt_ref[c, head_cols[h], :] * wk, jnp.where(first_row, wk, 0.0)],
                                  axis=0).astype(BF16)
            update = jnp.dot(lhs, k_ref[chunk_rows[c], head_cols[h]], preferred_element_type=F32)
            incr[c, h] = (b_edge, dec_max, update)

    states = [state_s[h] for h in range(HEADS)]
    ms = [m_s[h:h + 1, :] for h in range(HEADS)]
    partial = {}

    def numerators(c):
        for h in range(HEADS):
            st_b, s_inter, s_loc, carried = partial.pop((c, h))
            num_loc = jnp.dot(vt_ref[c, head_cols[h], :].astype(BF16), st_b, preferred_element_type=F32)
            out_ref[c, head_cols[h], :] = s_inter * carried + s_loc * num_loc

    for i, c in enumerate(order):
        ga = ga_ref[chunk_rows[c], :]
        for h in range(HEADS):
            b_row = b_rows[c, h]
            a_col = ga[:, ig0 + h:ig0 + h + 1]
            state, m_prev = states[h], ms[h]
            d = jnp.where(visible, b_row + a_col, -jnp.inf)
            m_loc = jnp.max(d, axis=0, keepdims=True)
            lhs = jnp.concatenate([k_ref[chunk_rows[c], head_cols[h]], state.astype(BF16)], axis=0)
            both = jnp.dot(lhs, qt_ref[c, head_cols[h], :], preferred_element_type=F32)
            st = both[:CHUNK] * jnp.exp2(d - m_loc)
            carried = both[CHUNK:]
            inter = b_row + m_prev
            m_j = jnp.maximum(inter, m_loc)
            w_inter = jnp.exp2(inter - m_j)
            w_loc = jnp.exp2(m_loc - m_j)
            nq = (w_inter * carried[HEAD_DIM:HEAD_DIM + 1]
                  + w_loc * jnp.sum(st, axis=0, keepdims=True))
            inv = 1.0 / jnp.maximum(jnp.abs(nq), jnp.exp2(-m_j))
            partial[c, h] = (st.astype(BF16), w_inter * inv, w_loc * inv, carried[:HEAD_DIM])

            b_edge, dec_max, update = incr[c, h]
            m_new = jnp.maximum(b_edge + m_prev, dec_max)
            states[h] = (jnp.exp2(b_edge + m_prev - m_new) * state
                         + jnp.exp2(dec_max - m_new) * update)
            ms[h] = m_new
        if i > 0:
            numerators(order[i - 1])
    numerators(order[-1])
    for h in range(HEADS):
        state_s[h] = states[h]
        m_s[h:h + 1, :] = ms[h]


def _reset_state(state_s, m_s):
    state_s[...] = jnp.zeros_like(state_s)
    m_s[...] = jnp.zeros_like(m_s)


def _mix_bwd_kernel(n_tiles, hn_ref, hprev_ref, hnext_ref, win_ref, sgun_ref, convw_ref, convb_ref, bdk_ref,
                    bdqt_ref, bdvt_ref, wgx_ref, wgm_ref, bgate_ref,
                    hbt_ref, elem_ref, vn_ref, qt_ref, k_ref, vt_ref, ga_ref, gr_ref,
                    state_s, m_s):
    j = pl.program_id(1)
    seq_tile = n_tiles - 1 - j

    @pl.when(j == 0)
    def _():
        _reset_state(state_s, m_s)

    _mlstm_prepare(seq_tile == 0, seq_tile == n_tiles - 1, hn_ref, hprev_ref, hnext_ref, win_ref, sgun_ref,
                   convw_ref, convb_ref, bdk_ref, bdqt_ref, bdvt_ref, wgx_ref, wgm_ref, bgate_ref,
                   elem_ref, vn_ref, qt_ref, k_ref, vt_ref, ga_ref, gr_ref)
    _mlstm_direction(True, hn_ref.shape[0] // CHUNK, qt_ref, k_ref, vt_ref, ga_ref, gr_ref,
                     state_s, m_s, hbt_ref)


def _mix_fwd_kernel(final_norm, h_ref, hbt_ref, elem_ref, vn_ref, qt_ref, k_ref, vt_ref,
                    ga_ref, gr_ref, sguw_ref, sgub_ref, mhn_ref, skip_ref, wout_ref,
                    g2_ref, wg_ref, wu_ref, wd_ref, gf_ref, o_ref,
                    state_s, m_s, hft_s, y_s):
    tile = h_ref.shape[0]

    @pl.when(pl.program_id(1) == 0)
    def _():
        _reset_state(state_s, m_s)

    _mlstm_direction(False, tile // CHUNK, qt_ref, k_ref, vt_ref, ga_ref, gr_ref, state_s, m_s, hft_s)

    for h in range(HEADS):
        cols = slice(h * HEAD_DIM, (h + 1) * HEAD_DIM)
        for c in range(tile // CHUNK):
            rows = slice(c * CHUNK, (c + 1) * CHUNK)
            mixed = jnp.dot(sguw_ref[h], vn_ref[rows, cols], preferred_element_type=F32) + sgub_ref[h]
            y_s[rows, cols] = (elem_ref[rows, cols] * mixed).astype(BF16)

    n_chunks = tile // CHUNK
    for c in range(n_chunks):
        rows = slice(c * CHUNK, (c + 1) * CHUNK)
        for h in range(HEADS):
            cols = slice(h * HEAD_DIM, (h + 1) * HEAD_DIM)
            sg_cols = slice(GROUP_WIDTH + h * HEAD_DIM, GROUP_WIDTH + (h + 1) * HEAD_DIM)
            xc_cols = slice(2 * GROUP_WIDTH + h * HEAD_DIM, 2 * GROUP_WIDTH + (h + 1) * HEAD_DIM)
            ht = hft_s[c, cols, :] + hbt_ref[c, cols, :]
            mu = jnp.mean(ht, axis=0, keepdims=True)
            hc = ht - mu
            var = jnp.mean(hc * hc, axis=0, keepdims=True)
            hn = (hc * lax.rsqrt(var + EPS) * mhn_ref[h]).T
            ym = (hn + skip_ref[:, cols] * elem_ref[rows, xc_cols]) * elem_ref[rows, sg_cols]
            y_s[rows, sg_cols] = ym.astype(BF16)

    h = h_ref[...] + jnp.dot(y_s[...], wout_ref[...], preferred_element_type=F32)

    hn = _rms_norm(h, g2_ref[...]).astype(BF16)
    x = h + 0.5 * _swiglu(hn, wg_ref, wu_ref, wd_ref)
    if final_norm:
        x = _rms_norm(x, gf_ref[...])
    o_ref[...] = x


def _compiler_params(grid_rank):
    return pltpu.CompilerParams(dimension_semantics=("arbitrary",) * grid_rank,
                                vmem_limit_bytes=VMEM_LIMIT_BYTES)


def _resident(shape, layer=None):
    nd = len(shape)
    if layer is None:
        return pl.BlockSpec(shape, lambda *_: (0,) * nd, pipeline_mode=pl.Buffered(1))
    return pl.BlockSpec((None,) + shape, lambda *_: (layer,) + (0,) * nd,
                        pipeline_mode=pl.Buffered(1))


def _ffn_in(x, l, g1, wg, wu, wd, g2):
    tokens, d = x.shape
    hidden = wg.shape[-1]
    row = lambda i: (i, 0)
    return pl.pallas_call(
        _ffn_in_kernel,
        grid=(tokens // FFN_TILE,),
        in_specs=[pl.BlockSpec((FFN_TILE, d), row),
                  _resident((1, d), l), _resident((d, hidden), l), _resident((d, hidden), l),
                  _resident((hidden, d), l), _resident((1, d), l)],
        out_specs=[pl.BlockSpec((FFN_TILE, d), row), pl.BlockSpec((FFN_TILE, d), row)],
        out_shape=[jax.ShapeDtypeStruct((tokens, d), F32), jax.ShapeDtypeStruct((tokens, d), BF16)],
        compiler_params=_compiler_params(1),
        name=f"ffn_in_{l}",
    )(x, g1, wg, wu, wd, g2)


def _state_scratch():
    return [pltpu.VMEM((HEADS, STATE_ROWS, HEAD_DIM), F32),
            pltpu.VMEM((8, 128), F32)]


def _tile_row(n_tiles, reverse):
    return lambda b, j: b * n_tiles + ((n_tiles - 1 - j) if reverse else j)


def _shared_specs(tile_row, tile):
    gw = GROUP_WIDTH
    chunks = tile // CHUNK
    rows2 = lambda b, j: (tile_row(b, j), 0)
    rows3 = lambda b, j: (tile_row(b, j), 0, 0)
    return [pl.BlockSpec((tile, 3 * gw), rows2),
            pl.BlockSpec((tile, gw), rows2),
            pl.BlockSpec((chunks, gw, CHUNK), rows3),
            pl.BlockSpec((tile, gw), rows2),
            pl.BlockSpec((chunks, gw, CHUNK), rows3),
            pl.BlockSpec((tile, GATE_LANES), rows2),
            pl.BlockSpec((chunks * GATE_ROWS, CHUNK), rows2)]


def _mix_bwd(hn, l, batch, win, sgun, mw):
    tokens, d = hn.shape
    tile = MIX_BWD_TILE
    n_tiles = tokens // batch // tile
    tile_row = _tile_row(n_tiles, True)
    gw = GROUP_WIDTH
    halo_per_tile = tile // HALO_ROWS
    last_halo = tokens // HALO_ROWS - 1
    n_diag = gw // MXU_WIDTH
    chunks = tile // CHUNK
    shared_shapes = [jax.ShapeDtypeStruct((tokens, 3 * gw), F32), jax.ShapeDtypeStruct((tokens, gw), BF16),
                     jax.ShapeDtypeStruct((tokens // CHUNK, gw, CHUNK), BF16),
                     jax.ShapeDtypeStruct((tokens, gw), BF16),
                     jax.ShapeDtypeStruct((tokens // CHUNK, gw, CHUNK), F32),
                     jax.ShapeDtypeStruct((tokens, GATE_LANES), F32),
                     jax.ShapeDtypeStruct((tokens // CHUNK * GATE_ROWS, CHUNK), F32)]
    return pl.pallas_call(
        functools.partial(_mix_bwd_kernel, n_tiles),
        grid=(batch, n_tiles),
        in_specs=[pl.BlockSpec((tile, d), lambda b, j: (tile_row(b, j), 0)),
                  pl.BlockSpec((HALO_ROWS, d), lambda b, j: (
                      jnp.maximum(tile_row(b, j) * halo_per_tile - 1, 0), 0)),
                  pl.BlockSpec((HALO_ROWS, d), lambda b, j: (
                      jnp.minimum((tile_row(b, j) + 1) * halo_per_tile, last_halo), 0)),
                  _resident((d, 4 * gw), l), _resident((HEADS, HEAD_DIM), l),
                  _resident((CONV_WIDTH, gw), l), _resident((1, gw), l),
                  _resident((n_diag, MXU_WIDTH, MXU_WIDTH), l),
                  _resident((n_diag, MXU_WIDTH, MXU_WIDTH), l),
                  _resident((n_diag, MXU_WIDTH, MXU_WIDTH), l),
                  _resident((gw, GATE_LANES), l), _resident((gw, GATE_LANES), l),
                  _resident((1, GATE_LANES), l)],
        out_specs=[pl.BlockSpec((chunks, gw, CHUNK), lambda b, j: (tile_row(b, j), 0, 0))]
                  + _shared_specs(tile_row, tile),
        out_shape=[jax.ShapeDtypeStruct((tokens // CHUNK, gw, CHUNK), F32)] + shared_shapes,
        scratch_shapes=_state_scratch(),
        compiler_params=_compiler_params(2),
        name=f"mix_bwd_{l}",
    )(hn, hn, hn, win, sgun, *mw)


def _mix_fwd(h, hbt, shared, l, batch, sguw, sgub, mhn, skip, wout,
             g2, wg, wu, wd, gf, final_norm):
    tokens, d = h.shape
    hidden = wg.shape[-1]
    n_tiles = tokens // batch // MIX_TILE
    tile_row = _tile_row(n_tiles, False)
    gw = GROUP_WIDTH
    chunks = MIX_TILE // CHUNK
    rows2 = lambda b, j: (tile_row(b, j), 0)
    return pl.pallas_call(
        functools.partial(_mix_fwd_kernel, final_norm),
        grid=(batch, n_tiles),
        in_specs=[pl.BlockSpec((MIX_TILE, d), rows2),
                  pl.BlockSpec((chunks, gw, CHUNK), lambda b, j: (tile_row(b, j), 0, 0))]
                 + _shared_specs(tile_row, MIX_TILE)
                 + [_resident((HEADS, CHUNK, CHUNK), l),
                    _resident((HEADS, CHUNK, HEAD_DIM), l), _resident((HEADS, HEAD_DIM, CHUNK), l),
                    _resident((1, gw), l), _resident((2 * gw, d), l),
                    _resident((1, d), l), _resident((d, hidden), l), _resident((d, hidden), l),
                    _resident((hidden, d), l), _resident((1, d))],
        out_specs=pl.BlockSpec((MIX_TILE, d), rows2),
        out_shape=jax.ShapeDtypeStruct((tokens, d), F32),
        scratch_shapes=_state_scratch() + [pltpu.VMEM((chunks, gw, CHUNK), F32),
                                           pltpu.VMEM((MIX_TILE, 2 * gw), BF16)],
        compiler_params=_compiler_params(2),
        name=f"mix_fwd_{l}",
    )(h, hbt, *shared, sguw, sgub, mhn, skip, wout, g2, wg, wu, wd, gf)


def _diag_blocks(w, transpose=False, dtype=BF16):
    depth, groups, bi, bo = w.shape
    rows = w.reshape(depth, groups * bi // MXU_WIDTH, MXU_WIDTH, bo)
    idx = jnp.arange(MXU_WIDTH)
    tiled = jnp.einsum('dnro,oc->dnrc', rows, (idx[None, :] % bo == jnp.arange(bo)[:, None]).astype(F32),
                       precision=lax.Precision.HIGHEST)
    dense = jnp.where(idx[:, None] // bi == idx[None, :] // bo, tiled, 0.0)
    if transpose:
        dense = jnp.swapaxes(dense, -1, -2)
    return dense.astype(dtype)


def _pack_gates(w_fwd, b_fwd, w_bwd, b_bwd):
    depth, rows, n = w_fwd.shape
    w = jnp.concatenate([w_fwd, w_bwd, jnp.zeros((depth, rows, GATE_LANES - 2 * n), F32)], axis=-1)
    b = jnp.concatenate([b_fwd, b_bwd, jnp.zeros((depth, GATE_LANES - 2 * n), F32)], axis=-1)
    return w, b.reshape(depth, 1, GATE_LANES)


def _fold_gate_weights(w_q, w_k, w_v, wgate):
    depth = wgate.shape[0]
    gw = GROUP_WIDTH

    def folded(w, rows):
        blocks = _diag_blocks(w, dtype=F32)
        wg = wgate[:, rows:rows + gw].reshape(depth, gw // MXU_WIDTH, MXU_WIDTH, GATE_LANES)
        return jnp.einsum('dnrc,dnck->dnrk', blocks, wg,
                          precision=lax.Precision.HIGHEST).reshape(depth, gw, GATE_LANES)

    wgx = folded(w_q, 0) + folded(w_k, gw) * (HEAD_DIM ** -0.5)
    return wgx.astype(BF16), folded(w_v, 2 * gw).astype(BF16)


def kernel(x, ffn1_norm, ffn1_w_gate, ffn1_w_up, ffn1_w_down, mix_norm, w_in, sgu_norm, sgu_w, sgu_b, conv_w, conv_b, w_q, w_k, w_v, gate_w_fwd, gate_b_fwd, gate_w_bwd, gate_b_bwd, mh_norm, mlstm_skip, w_out, ffn2_norm, ffn2_w_gate, ffn2_w_up, ffn2_w_down, final_norm):
    batch, seq, d = x.shape
    depth = w_in.shape[0]
    assert seq % MIX_TILE == 0 and seq % MIX_BWD_TILE == 0 and (batch * seq) % FFN_TILE == 0
    assert sgu_w.shape[1:] == (HEADS, CHUNK, CHUNK) and mh_norm.shape[1:] == (HEADS, HEAD_DIM)
    assert gate_w_fwd.shape[1:] == (3 * GROUP_WIDTH, 2 * HEADS)

    bf = lambda w: w.astype(BF16)
    row3 = lambda g: g.reshape(depth, 1, -1)
    f1 = (row3(ffn1_norm), bf(ffn1_w_gate), bf(ffn1_w_up), bf(ffn1_w_down))
    f2 = (row3(ffn2_norm), bf(ffn2_w_gate), bf(ffn2_w_up), bf(ffn2_w_down))
    wgate, bgate = _pack_gates(gate_w_fwd, gate_b_fwd, gate_w_bwd, gate_b_bwd)
    convb = row3(conv_b)
    wgx, wgm = _fold_gate_weights(w_q, w_k, w_v, wgate)
    mw = (conv_w, convb, _diag_blocks(w_k), _diag_blocks(w_q, transpose=True),
          _diag_blocks(w_v, transpose=True), wgx, wgm, bgate)
    sgub = jnp.broadcast_to(sgu_b[..., None], sgu_b.shape + (HEAD_DIM,))
    mhn = jnp.broadcast_to(mh_norm[..., None], mh_norm.shape + (CHUNK,))
    gfin = final_norm.reshape(1, d)

    xt = x.reshape(batch * seq, d)
    for l in range(depth):
        h, hn = _ffn_in(xt, l, *f1, row3(mix_norm))
        hbt, *shared = _mix_bwd(hn, l, batch, bf(w_in), sgu_norm, mw)
        xt = _mix_fwd(h, hbt, shared, l, batch, bf(sgu_w), sgub, mhn,
                      row3(mlstm_skip), bf(w_out), *f2, gfin, final_norm=(l == depth - 1))
    return xt.reshape(batch, seq, d)
```
